```python
import math
import jax, jax.numpy as jnp
from jax import lax
import numpy as np

D_MODEL = 2048
BATCH = 1
SEQ = 8192
DEPTH = 4

DIL_PAIRS = ((128, 1), (512, 4), (2048, 16))
N_GROUPS = len(DIL_PAIRS)
A_HEADS = 8
A_HEAD_DIM = 128
A_WIDTH = A_HEADS * A_HEAD_DIM
A_SPAN = 128
A_BLOCK = 128
G_HEADS = 4
G_DK = D_MODEL // 2
G_DV = D_MODEL
G_HK = G_DK // G_HEADS
G_HV = G_DV // G_HEADS
G_RANK = 16
G_TAU = 16.0
G_CHUNK = 64
N_BUCKETS = 32
MAX_DIST = 2048
D_FF = 5632
CONV_W = 3
LN_EPS = 1e-5
ALPHA = (2 * DEPTH) ** 0.25
BETA = (8 * DEPTH) ** -0.25
IN_SIZES = (A_WIDTH,) * (3 * N_GROUPS) + (G_DK, G_DK, G_DV, G_DV, G_RANK, D_MODEL, D_MODEL)
IN_COLS = sum(IN_SIZES)

kernel_name = 'hybrid_dilated_gla_convffn'


def _layernorm(x, g, b):
    xf = x.astype(jnp.float32)
    mu = jnp.mean(xf, axis=-1, keepdims=True)
    var = jnp.mean(jnp.square(xf - mu), axis=-1, keepdims=True)
    return ((xf - mu) * lax.rsqrt(var + LN_EPS) * g + b).astype(x.dtype)


def _t5_bucket(dist):
    max_exact = N_BUCKETS // 2
    df = jnp.maximum(dist, 1).astype(jnp.float32)
    large = max_exact + (jnp.log(df / max_exact) / math.log(MAX_DIST / max_exact)
                         * (N_BUCKETS - max_exact)).astype(jnp.int32)
    return jnp.where(dist < max_exact, dist, jnp.minimum(large, N_BUCKETS - 1))


def _dilated_group(q, k, v, bias_tab, dilation):
    b, s, h, e = q.shape
    n_sub = s // dilation
    nb = -(-n_sub // A_BLOCK)
    pad = nb * A_BLOCK - n_sub

    def to_blocks(t):
        t = t.reshape(b, n_sub, dilation, h, e).transpose(0, 2, 1, 3, 4)
        t = jnp.pad(t, ((0, 0), (0, 0), (0, pad), (0, 0), (0, 0)))
        return t.reshape(b, dilation, nb, A_BLOCK, h, e)

    def with_prev(t):
        prev = jnp.concatenate([jnp.zeros_like(t[:, :, :1]), t[:, :, :-1]], axis=2)
        return jnp.concatenate([prev, t], axis=3)

    qb = to_blocks(q)
    kb = with_prev(to_blocks(k))
    vb = with_prev(to_blocks(v))
    qi = jnp.arange(A_BLOCK)[:, None]
    ci = jnp.arange(2 * A_BLOCK)[None, :]
    off = A_BLOCK + qi - ci
    valid = (off >= 0) & (off <= A_SPAN)
    first = (jnp.arange(nb) == 0)[:, None, None] & (ci < A_BLOCK)[None]
    mask = valid[None] & ~first
    bucket = _t5_bucket(dilation * jnp.clip(off, 0, A_SPAN))
    bias = jnp.transpose(bias_tab[bucket], (2, 0, 1)).astype(jnp.float32)
    sc = jnp.einsum('brnqhe,brnkhe->brnhqk', qb, kb).astype(jnp.float32) * (e ** -0.5) + bias
    sc = jnp.where(mask[None, None, :, None], sc, -1e30)
    m = jnp.max(sc, axis=-1, keepdims=True)
    p = jnp.exp(sc - m)
    l = jnp.sum(p, axis=-1, keepdims=True)
    o = jnp.einsum('brnhqk,brnkhe->brnqhe', (p / l).astype(v.dtype), vb)
    lse = (m + jnp.log(l))[..., 0]
    o = o.reshape(b, dilation, nb * A_BLOCK, h, e)[:, :, :n_sub]
    o = o.transpose(0, 2, 1, 3, 4).reshape(b, s, h, e)
    lse = lse.transpose(0, 1, 2, 4, 3).reshape(b, dilation, nb * A_BLOCK, h)[:, :, :n_sub]
    lse = lse.transpose(0, 2, 1, 3).reshape(b, s, h)
    return o, lse


def _gla(q, k, v, log_a):
    b, s, h, dk = q.shape
    dv = v.shape[-1]
    n = s // G_CHUNK
    q, k, log_a = (t.astype(jnp.float32).reshape(b, n, G_CHUNK, h, dk) for t in (q, k, log_a))
    v = v.astype(jnp.float32).reshape(b, n, G_CHUNK, h, dv)
    cum = jnp.cumsum(log_a, axis=2)
    last = cum[:, :, -1]
    q_dec = q * jnp.exp(cum)
    k_inv = k * jnp.exp(-cum)
    k_end = k * jnp.exp(last[:, :, None] - cum)
    causal = jnp.tril(jnp.ones((G_CHUNK, G_CHUNK), dtype=bool))
    att = jnp.where(causal, jnp.einsum('bnihd,bnjhd->bnhij', q_dec, k_inv), 0.0)
    o_intra = jnp.einsum('bnhij,bnjhe->bnihe', att, v)

    def step(state, inp):
        qc, kc, vc, lc = inp
        o_c = jnp.einsum('bihd,bhde->bihe', qc, state)
        state = state * jnp.exp(lc)[..., None] + jnp.einsum('bjhd,bjhe->bhde', kc, vc)
        return state, o_c

    xs = tuple(jnp.moveaxis(t, 1, 0) for t in (q_dec, k_end, v, last))
    _, o_inter = lax.scan(step, jnp.zeros((b, h, dk, dv), jnp.float32), xs)
    o = o_intra + jnp.moveaxis(o_inter, 0, 1)
    return o.reshape(b, s, h, dv)


def _mixer(x, w_in, gla_gate_w, gla_gate_b, gla_norm_g, w_proj_a, w_proj_b, w_out, rel_bias):
    b, s, _ = x.shape
    cuts = np.cumsum(IN_SIZES)[:-1].tolist()
    parts = jnp.split(x @ w_in, cuts, axis=-1)
    a_parts = parts[:3 * N_GROUPS]
    gq, gk, gv, gr, g_low, m_a, m_b = parts[3 * N_GROUPS:]
    outs, lses = [], []
    for g, (_, dil) in enumerate(DIL_PAIRS):
        qg, kg, vg = (t.reshape(b, s, A_HEADS, A_HEAD_DIM) for t in a_parts[3 * g:3 * g + 3])
        o, lse = _dilated_group(qg, kg, vg, rel_bias[:, g * A_HEADS:(g + 1) * A_HEADS], dil)
        outs.append(o)
        lses.append(lse)
    wts = jax.nn.softmax(jnp.stack(lses), axis=0)
    ya = jnp.einsum('gbsh,gbshe->bshe', wts, jnp.stack(outs).astype(jnp.float32))
    ya = ya.reshape(b, s, A_WIDTH).astype(x.dtype) @ w_proj_a
    log_a = jax.nn.log_sigmoid((g_low @ gla_gate_w + gla_gate_b).astype(jnp.float32)) / G_TAU
    o = _gla(gq.reshape(b, s, G_HEADS, G_HK) * (G_HK ** -0.5),
             gk.reshape(b, s, G_HEADS, G_HK),
             gv.reshape(b, s, G_HEADS, G_HV),
             log_a.reshape(b, s, G_HEADS, G_HK))
    o = o * lax.rsqrt(jnp.mean(o * o, axis=-1, keepdims=True) + LN_EPS) * gla_norm_g
    yb = (o.reshape(b, s, G_DV).astype(x.dtype) * jax.nn.silu(gr)) @ w_proj_b
    y = jax.nn.sigmoid(m_a) * ya + jax.nn.sigmoid(m_b) * yb
    return y @ w_out


def _conv_ffn(x, w_gate, w_up, conv_w, conv_b, w_down):
    s = x.shape[1]
    g = x @ w_gate
    gp = jnp.pad(g, ((0, 0), (CONV_W - 1, 0), (0, 0)))
    gc = conv_b
    for i in range(CONV_W):
        gc = gc + conv_w[i] * gp[:, i:i + s]
    return (jax.nn.silu(gc) * (x @ w_up)) @ w_down


def setup_inputs(seed: int = 0) -> dict:
    key = jax.random.key(seed)
    ks = jax.random.split(key, 20)

    def nrm(k, shape, scale):
        return jax.random.normal(k, shape, jnp.float32) * scale

    L = DEPTH
    return {
        'x': nrm(ks[0], (BATCH, SEQ, D_MODEL), 1.0),
        'w_in': nrm(ks[1], (L, D_MODEL, IN_COLS), D_MODEL ** -0.5),
        'gla_gate_w': nrm(ks[2], (L, G_RANK, G_DK), G_RANK ** -0.5),
        'gla_gate_b': nrm(ks[3], (L, G_DK), 0.1),
        'gla_norm_g': 1.0 + nrm(ks[4], (L, G_HV), 0.02),
        'w_proj_a': nrm(ks[5], (L, A_WIDTH, D_MODEL), A_WIDTH ** -0.5),
        'w_proj_b': nrm(ks[6], (L, G_DV, D_MODEL), G_DV ** -0.5),
        'w_out': nrm(ks[7], (L, D_MODEL, D_MODEL), BETA * D_MODEL ** -0.5),
        'rel_bias': nrm(ks[8], (N_BUCKETS, N_GROUPS * A_HEADS), 0.5),
        'ln1_g': 1.0 + nrm(ks[9], (L, D_MODEL), 0.02),
        'ln1_b': nrm(ks[10], (L, D_MODEL), 0.02),
        'ffn_w_gate': nrm(ks[11], (L, D_MODEL, D_FF), D_MODEL ** -0.5),
        'ffn_w_up': nrm(ks[12], (L, D_MODEL, D_FF), D_MODEL ** -0.5),
        'ffn_conv_w': nrm(ks[13], (L, CONV_W, D_FF), CONV_W ** -0.5),
        'ffn_conv_b': nrm(ks[14], (L, D_FF), 0.02),
        'ffn_w_down': nrm(ks[15], (L, D_FF, D_MODEL), BETA * D_FF ** -0.5),
        'ln2_g': 1.0 + nrm(ks[16], (L, D_MODEL), 0.02),
        'ln2_b': nrm(ks[17], (L, D_MODEL), 0.02),
    }


def reference(x, w_in, gla_gate_w, gla_gate_b, gla_norm_g, w_proj_a, w_proj_b, w_out, rel_bias,
              ln1_g, ln1_b, ffn_w_gate, ffn_w_up, ffn_conv_w, ffn_conv_b, ffn_w_down, ln2_g, ln2_b):
    for i in range(DEPTH):
        mix = _mixer(x, w_in[i], gla_gate_w[i], gla_gate_b[i], gla_norm_g[i],
                     w_proj_a[i], w_proj_b[i], w_out[i], rel_bias)
        x = _layernorm(ALPHA * x + mix, ln1_g[i], ln1_b[i])
        ffn = _conv_ffn(x, ffn_w_gate[i], ffn_w_up[i], ffn_conv_w[i], ffn_conv_b[i], ffn_w_down[i])
        x = _layernorm(ALPHA * x + ffn, ln2_g[i], ln2_b[i])
    return x
```

```python
import functools
import math

import jax
import jax.numpy as jnp
from jax import lax
from jax.experimental import pallas as pl
from jax.experimental.pallas import tpu as pltpu

F32 = jnp.float32
BF16 = jnp.bfloat16

DEPTH = 4
DILATIONS = (1, 4, 16)
A_HEADS = 8
A_HEAD_DIM = 128
A_WIDTH = A_HEADS * A_HEAD_DIM
A_SPAN = 128
A_BLOCK = 128
G_HEADS = 4
G_RANK = 16
G_TAU = 16.0
G_CHUNK = 64
N_BUCKETS = 32
MAX_DIST = 2048
CONV_W = 3
LN_EPS = 1e-5
ALPHA = (2 * DEPTH) ** 0.25
NEG = -1e30
LANES = 128
VMEM_LIMIT = 56 << 20


def _params(*sem):
    return pltpu.CompilerParams(dimension_semantics=sem, vmem_limit_bytes=VMEM_LIMIT)


def _nt(a, b):
    return lax.dot_general(a, b, (((1,), (1,)), ((), ())), preferred_element_type=F32)


def _tn(a, b):
    return lax.dot_general(a, b, (((0,), (0,)), ((), ())), preferred_element_type=F32)


def _dot(a, b):
    return jnp.dot(a, b, preferred_element_type=F32)


def _sigmoid(x):
    return 1.0 / (1.0 + jnp.exp(-x))


def _split2(a):
    hi = a.astype(BF16)
    lo = (a - hi.astype(F32)).astype(BF16)
    return hi, lo


def _split3(a):
    hi = a.astype(BF16)
    r = a - hi.astype(F32)
    mid = r.astype(BF16)
    lo = (r - mid.astype(F32)).astype(BF16)
    return hi, mid, lo


def _layernorm(z, g, b):
    mu = jnp.mean(z, axis=-1, keepdims=True)
    zc = z - mu
    var = jnp.mean(zc * zc, axis=-1, keepdims=True)
    return zc * lax.rsqrt(var + LN_EPS) * g + b


def _proj_kernel(x_ref, w_ref, o_ref, *, act):
    acc = _dot(x_ref[...], w_ref[...])
    if act == "sigmoid":
        acc = _sigmoid(acc)
    o_ref[0] = acc.astype(o_ref.dtype)


def _proj(xb, w, d, out_dtype, act=None, tm=1024, tn=1024):
    s, dm = xb.shape
    n = w.shape[1]
    n_sub = s // d
    tm = min(tm, n_sub)
    tn = min(tn, n)
    x2 = xb.reshape(n_sub, d * dm)
    return pl.pallas_call(
        functools.partial(_proj_kernel, act=act),
        grid=(n // tn, d, n_sub // tm),
        in_specs=[pl.BlockSpec((tm, dm), lambda j, r, m: (m, r)),
                  pl.BlockSpec((dm, tn), lambda j, r, m: (0, j))],
        out_specs=pl.BlockSpec((1, tm, tn), lambda j, r, m: (r, m, j)),
        out_shape=jax.ShapeDtypeStruct((d, n_sub, n), out_dtype),
        compiler_params=_params("parallel", "parallel", "parallel"),
        name=f"proj_d{d}_{n}",
    )(x2, w)


def _gate_kernel(x_ref, wl_ref, gw_ref, gb_ref, cum_ref, *, n_chunks):
    g = _dot(x_ref[...], wl_ref[...])
    g_hi, g_lo = _split2(g)
    w_hi, w_lo = _split2(gw_ref[...])
    z = _dot(g_hi, w_hi) + _dot(g_hi, w_lo) + _dot(g_lo, w_hi) + gb_ref[...]
    log_a = (jnp.minimum(z, 0.0) - jnp.log(1.0 + jnp.exp(-jnp.abs(z)))) * (1.0 / G_TAU)
    row = lax.broadcasted_iota(jnp.int32, (G_CHUNK, G_CHUNK), 0)
    col = lax.broadcasted_iota(jnp.int32, (G_CHUNK, G_CHUNK), 1)
    tril = jnp.where(row >= col, 1.0, 0.0).astype(BF16)
    for c in range(n_chunks):
        hi, mid, lo = _split3(log_a[c * G_CHUNK:(c + 1) * G_CHUNK])
        cum_ref[c * G_CHUNK:(c + 1) * G_CHUNK, :] = _dot(tril, hi) + _dot(tril, mid) + _dot(tril, lo)


def _gate(xb, w_low, gate_w, gate_b, tm=512):
    s, dm = xb.shape
    dk = gate_w.shape[1]
    tm = min(tm, s)
    return pl.pallas_call(
        functools.partial(_gate_kernel, n_chunks=tm // G_CHUNK),
        grid=(s // tm,),
        in_specs=[pl.BlockSpec((tm, dm), lambda m: (m, 0)),
                  pl.BlockSpec((dm, LANES), lambda m: (0, 0)),
                  pl.BlockSpec((LANES, dk), lambda m: (0, 0)),
                  pl.BlockSpec((1, dk), lambda m: (0, 0))],
        out_specs=pl.BlockSpec((tm, dk), lambda m: (m, 0)),
        out_shape=jax.ShapeDtypeStruct((s, dk), F32),
        compiler_params=_params("parallel"),
        name="gla_gate",
    )(xb, w_low, gate_w, gate_b)


def _gla_kernel(q_ref, k_ref, v_ref, c_ref, r_ref, ng_ref, o_ref, st_ref, *, n_chunks, scale):
    @pl.when(pl.program_id(1) == 0)
    def _():
        st_ref[...] = jnp.zeros_like(st_ref)

    row = lax.broadcasted_iota(jnp.int32, (G_CHUNK, G_CHUNK), 0)
    col = lax.broadcasted_iota(jnp.int32, (G_CHUNK, G_CHUNK), 1)
    causal = row >= col
    ng = ng_ref[...]
    for c in range(n_chunks):
        sl = slice(c * G_CHUNK, (c + 1) * G_CHUNK)
        cum = c_ref[sl, :]
        last = cum[G_CHUNK - 1:G_CHUNK, :]
        q = q_ref[0, sl, :].astype(F32) * scale
        k = k_ref[0, sl, :].astype(F32)
        v = v_ref[0, sl, :].astype(BF16)
        qd = (q * jnp.exp(cum)).astype(BF16)
        ki = (k * jnp.exp(-cum)).astype(BF16)
        ke = (k * jnp.exp(last - cum)).astype(BF16)
        att = jnp.where(causal, _nt(qd, ki), 0.0)
        st = st_ref[...]
        o = _dot(att.astype(BF16), v) + _nt(qd, st.astype(BF16))
        st_ref[...] = st * jnp.exp(last) + _tn(v, ke)
        ms = jnp.mean(o * o, axis=-1, keepdims=True)
        on = o * lax.rsqrt(ms + LN_EPS) * ng
        gr = r_ref[0, sl, :].astype(F32)
        o_ref[sl, :] = (on * (gr * _sigmoid(gr))).astype(o_ref.dtype)


def _gla(gla_in, cum, norm_g, tg=512):
    s, dk = cum.shape
    dv = (gla_in.shape[2] - 2 * dk) // 2
    hk, hv = dk // G_HEADS, dv // G_HEADS
    tg = min(tg, s)
    kq = dk // hk
    kv = 2 * dk // hv
    kr = (2 * dk + dv) // hv
    return pl.pallas_call(
        functools.partial(_gla_kernel, n_chunks=tg // G_CHUNK, scale=hk ** -0.5),
        grid=(G_HEADS, s // tg),
        in_specs=[pl.BlockSpec((1, tg, hk), lambda h, j: (0, j, h)),
                  pl.BlockSpec((1, tg, hk), lambda h, j: (0, j, kq + h)),
                  pl.BlockSpec((1, tg, hv), lambda h, j: (0, j, kv + h)),
                  pl.BlockSpec((tg, hk), lambda h, j: (j, h)),
                  pl.BlockSpec((1, tg, hv), lambda h, j: (0, j, kr + h)),
                  pl.BlockSpec((1, hv), lambda h, j: (0, 0))],
        out_specs=pl.BlockSpec((tg, hv), lambda h, j: (j, h)),
        out_shape=jax.ShapeDtypeStruct((s, dv), BF16),
        scratch_shapes=[pltpu.VMEM((hv, hk), F32)],
        compiler_params=_params("parallel", "arbitrary"),
        name="gla",
    )(gla_in, gla_in, gla_in, cum, gla_in, norm_g)


def _attn_kernel(q_ref, kc_ref, vc_ref, kp_ref, vp_ref, b_ref, o_ref, lse_ref, *, nblk, scale):
    first = pl.program_id(1) == 0
    lane = lax.broadcasted_iota(jnp.int32, (A_BLOCK, LANES), 1)
    col = lax.broadcasted_iota(jnp.int32, (A_BLOCK, 2 * A_BLOCK), 1)
    for b in range(nblk):
        rows = slice(b * A_BLOCK, (b + 1) * A_BLOCK)
        prev = slice((b - 1) * A_BLOCK, b * A_BLOCK)
        lse_tile = jnp.zeros((A_BLOCK, LANES), F32)
        for h in range(A_HEADS):
            cs = slice(h * A_HEAD_DIM, (h + 1) * A_HEAD_DIM)
            q = q_ref[0, rows, cs]
            if b == 0:
                kp, vp = kp_ref[0, :, cs], vp_ref[0, :, cs]
            else:
                kp, vp = kc_ref[0, prev, cs], vc_ref[0, prev, cs]
            kk = jnp.concatenate([kp, kc_ref[0, rows, cs]], axis=0)
            vv = jnp.concatenate([vp, vc_ref[0, rows, cs]], axis=0)
            sc = _nt(q, kk) * scale + b_ref[h]
            if b == 0:
                sc = jnp.where(jnp.logical_and(first, col < A_BLOCK), NEG, sc)
            m = jnp.max(sc, axis=-1, keepdims=True)
            p = jnp.exp(sc - m)
            l = jnp.sum(p, axis=-1, keepdims=True)
            o_ref[0, rows, cs] = (_dot(p.astype(BF16), vv) / l).astype(o_ref.dtype)
            lse_tile = jnp.where(lane == h, m + jnp.log(l), lse_tile)
        lse_ref[0, rows, :] = lse_tile


def _attention(qkv, bias, tq=512):
    d, n_sub, _ = qkv.shape
    tq = min(tq, n_sub)
    nblk = tq // A_BLOCK
    cur = lambda c: pl.BlockSpec((1, tq, A_WIDTH), lambda r, j: (r, j, c))
    prv = lambda c: pl.BlockSpec((1, A_BLOCK, A_WIDTH),
                                 lambda r, j: (r, jnp.maximum(j * nblk - 1, 0), c))
    return pl.pallas_call(
        functools.partial(_attn_kernel, nblk=nblk, scale=A_HEAD_DIM ** -0.5),
        grid=(d, n_sub // tq),
        in_specs=[cur(0), cur(1), cur(2), prv(1), prv(2),
                  pl.BlockSpec((A_HEADS, A_BLOCK, 2 * A_BLOCK), lambda r, j: (0, 0, 0))],
        out_specs=[pl.BlockSpec((1, tq, A_WIDTH), lambda r, j: (r, j, 0)),
                   pl.BlockSpec((1, tq, LANES), lambda r, j: (r, j, 0))],
        out_shape=[jax.ShapeDtypeStruct((d, n_sub, A_WIDTH), F32),
                   jax.ShapeDtypeStruct((d, n_sub, LANES), F32)],
        compiler_params=_params("parallel", "parallel"),
        name=f"attn_d{d}",
    )(qkv, qkv, qkv, qkv, qkv, bias)


def _combine_kernel(o0_ref, o1_ref, o2_ref, l0_ref, l1_ref, l2_ref, w_ref, y_ref):
    l0, l1, l2 = l0_ref[...], l1_ref[0], l2_ref[0]
    m = jnp.maximum(jnp.maximum(l0, l1), l2)
    e0, e1, e2 = jnp.exp(l0 - m), jnp.exp(l1 - m), jnp.exp(l2 - m)
    inv = 1.0 / (e0 + e1 + e2)
    w0, w1, w2 = e0 * inv, e1 * inv, e2 * inv
    pieces = []
    for h in range(A_HEADS):
        cs = slice(h * A_HEAD_DIM, (h + 1) * A_HEAD_DIM)
        ya = (w0[:, h:h + 1] * o0_ref[:, cs] + w1[:, h:h + 1] * o1_ref[0, :, cs]
              + w2[:, h:h + 1] * o2_ref[0, :, cs])
        pieces.append(ya.astype(BF16))
    y_ref[...] = _dot(jnp.concatenate(pieces, axis=1), w_ref[...]).astype(y_ref.dtype)


def _combine(outs, lses, w_proj_a, tm=512):
    (o0, o1, o2), (l0, l1, l2) = outs, lses
    d1, d2 = DILATIONS[1], DILATIONS[2]
    s = o0.shape[1]
    n2 = s // d2
    dm = w_proj_a.shape[1]
    tm = min(tm, n2)
    q = d2 // d1
    o0r = o0.reshape(n2, d2 * A_WIDTH)
    l0r = l0.reshape(n2, d2 * LANES)
    o1r = o1.reshape(d1, n2, q * A_WIDTH)
    l1r = l1.reshape(d1, n2, q * LANES)
    y = pl.pallas_call(
        _combine_kernel,
        grid=(d2, n2 // tm),
        in_specs=[pl.BlockSpec((tm, A_WIDTH), lambda r, m: (m, r)),
                  pl.BlockSpec((1, tm, A_WIDTH), lambda r, m: (r % d1, m, r // d1)),
                  pl.BlockSpec((1, tm, A_WIDTH), lambda r, m: (r, m, 0)),
                  pl.BlockSpec((tm, LANES), lambda r, m: (m, r)),
                  pl.BlockSpec((1, tm, LANES), lambda r, m: (r % d1, m, r // d1)),
                  pl.BlockSpec((1, tm, LANES), lambda r, m: (r, m, 0)),
                  pl.BlockSpec((A_WIDTH, dm), lambda r, m: (0, 0))],
        out_specs=pl.BlockSpec((tm, dm), lambda r, m: (m, r)),
        out_shape=jax.ShapeDtypeStruct((n2, d2 * dm), F32),
        compiler_params=_params("parallel", "parallel"),
        name="combine_proj_a",
    )(o0r, o1r, o2, l0r, l1r, l2, w_proj_a)
    return y.reshape(s, dm)


def _merge_kernel(a_ref, w_ref, ya_ref, sa_ref, sb_ref, y_ref):
    yb = _dot(a_ref[...], w_ref[...])
    y = sa_ref[0].astype(F32) * ya_ref[...] + sb_ref[0].astype(F32) * yb
    y_ref[...] = y.astype(y_ref.dtype)


def _merge(a, w_proj_b, ya, gates, tm=1024, tn=1024):
    s, dv = a.shape
    dm = w_proj_b.shape[1]
    tm, tn = min(tm, s), min(tn, dm)
    nb = dm // tn
    return pl.pallas_call(
        _merge_kernel,
        grid=(nb, s // tm),
        in_specs=[pl.BlockSpec((tm, dv), lambda j, m: (m, 0)),
                  pl.BlockSpec((dv, tn), lambda j, m: (0, j)),
                  pl.BlockSpec((tm, tn), lambda j, m: (m, j)),
                  pl.BlockSpec((1, tm, tn), lambda j, m: (0, m, j)),
                  pl.BlockSpec((1, tm, tn), lambda j, m: (0, m, nb + j))],
        out_specs=pl.BlockSpec((tm, tn), lambda j, m: (m, j)),
        out_shape=jax.ShapeDtypeStruct((s, dm), BF16),
        compiler_params=_params("parallel", "parallel"),
        name="proj_b_merge",
    )(a, w_proj_b, ya, gates, gates)


def _out_ln_kernel(y_ref, w_ref, x_ref, g_ref, b_ref, xo_ref, xb_ref):
    z = ALPHA * x_ref[...] + _dot(y_ref[...], w_ref[...])
    xn = _layernorm(z, g_ref[...], b_ref[...])
    xo_ref[...] = xn
    xb_ref[...] = xn.astype(BF16)


def _out_ln(y, w_out, x, g, b, tm=256):
    s, dm = x.shape
    tm = min(tm, s)
    row = pl.BlockSpec((tm, dm), lambda m: (m, 0))
    vec = pl.BlockSpec((1, dm), lambda m: (0, 0))
    return pl.pallas_call(
        _out_ln_kernel,
        grid=(s // tm,),
        in_specs=[row, pl.BlockSpec((dm, dm), lambda m: (0, 0)), row, vec, vec],
        out_specs=[row, row],
        out_shape=[jax.ShapeDtypeStruct((s, dm), F32), jax.ShapeDtypeStruct((s, dm), BF16)],
        compiler_params=_params("parallel"),
        name="out_proj_ln",
    )(y, w_out, x, g, b)


def _ffn1_kernel(x_ref, wg_ref, wu_ref, cw_ref, cb_ref, h_ref, halo_ref):
    tm = x_ref.shape[0]

    @pl.when(pl.program_id(1) == 0)
    def _():
        halo_ref[...] = jnp.zeros_like(halo_ref)

    x = x_ref[...]
    g = _dot(x, wg_ref[...])
    u = _dot(x, wu_ref[...])
    halo = halo_ref[...]
    row = lax.broadcasted_iota(jnp.int32, g.shape, 0)
    g1 = jnp.where(row == 0, halo[7:8], pltpu.roll(g, 1, axis=0))
    g2 = jnp.where(row == 0, halo[6:7], jnp.where(row == 1, halo[7:8], pltpu.roll(g, 2, axis=0)))
    halo_ref[...] = g[tm - 8:tm]
    cw = cw_ref[...]
    gc = cb_ref[...] + cw[0:1] * g2 + cw[1:2] * g1 + cw[2:3] * g
    h_ref[...] = (gc * _sigmoid(gc) * u).astype(h_ref.dtype)


def _ffn1(xb, w_gate, w_up, conv_w, conv_b, tm=1024, tn=512):
    s, dm = xb.shape
    dff = w_gate.shape[1]
    tm = min(tm, s)
    wspec = pl.BlockSpec((dm, tn), lambda j, m: (0, j))
    return pl.pallas_call(
        _ffn1_kernel,
        grid=(dff // tn, s // tm),
        in_specs=[pl.BlockSpec((tm, dm), lambda j, m: (m, 0)), wspec, wspec,
                  pl.BlockSpec((CONV_W, tn), lambda j, m: (0, j)),
                  pl.BlockSpec((1, tn), lambda j, m: (0, j))],
        out_specs=pl.BlockSpec((tm, tn), lambda j, m: (m, j)),
        out_shape=jax.ShapeDtypeStruct((s, dff), BF16),
        scratch_shapes=[pltpu.VMEM((8, tn), F32)],
        compiler_params=_params("parallel", "arbitrary"),
        name="ffn_gate_up_conv",
    )(xb, w_gate, w_up, conv_w, conv_b)


def _ffn2_kernel(h_ref, w_ref, x_ref, g_ref, b_ref, xo_ref, xb_ref, acc_ref):
    k = pl.program_id(1)
    part = _dot(h_ref[...], w_ref[...])

    @pl.when(k == 0)
    def _():
        acc_ref[...] = part

    @pl.when(k > 0)
    def _():
        acc_ref[...] += part

    @pl.when(k == pl.num_programs(1) - 1)
    def _():
        xn = _layernorm(ALPHA * x_ref[...] + acc_ref[...], g_ref[...], b_ref[...])
        xo_ref[...] = xn
        xb_ref[...] = xn.astype(BF16)


def _ffn2(h, w_down, x, g, b, tm=512, tk=512):
    s, dm = x.shape
    dff = h.shape[1]
    tm = min(tm, s)
    row = pl.BlockSpec((tm, dm), lambda m, k: (m, 0))
    vec = pl.BlockSpec((1, dm), lambda m, k: (0, 0))
    return pl.pallas_call(
        _ffn2_kernel,
        grid=(s // tm, dff // tk),
        in_specs=[pl.BlockSpec((tm, tk), lambda m, k: (m, k)),
                  pl.BlockSpec((tk, dm), lambda m, k: (k, 0)), row, vec, vec],
        out_specs=[row, row],
        out_shape=[jax.ShapeDtypeStruct((s, dm), F32), jax.ShapeDtypeStruct((s, dm), BF16)],
        scratch_shapes=[pltpu.VMEM((tm, dm), F32)],
        compiler_params=_params("parallel", "arbitrary"),
        name="ffn_down_ln",
    )(h, w_down, x, g, b)


def _band_bias(rel_bias_group, dilation):
    qi = jnp.arange(A_BLOCK)[:, None]
    ci = jnp.arange(2 * A_BLOCK)[None, :]
    off = A_BLOCK + qi - ci
    valid = (off >= 0) & (off <= A_SPAN)
    dist = dilation * jnp.clip(off, 0, A_SPAN)
    max_exact = N_BUCKETS // 2
    df = jnp.maximum(dist, 1).astype(F32)
    large = max_exact + (jnp.log(df / max_exact) / math.log(MAX_DIST / max_exact)
                         * (N_BUCKETS - max_exact)).astype(jnp.int32)
    bucket = jnp.where(dist < max_exact, dist, jnp.minimum(large, N_BUCKETS - 1))
    bias = jnp.transpose(rel_bias_group[bucket], (2, 0, 1)).astype(F32)
    return jnp.where(valid[None], bias, NEG)


def kernel(x, w_in, gla_gate_w, gla_gate_b, gla_norm_g, w_proj_a, w_proj_b, w_out, rel_bias,
           ln1_g, ln1_b, ffn_w_gate, ffn_w_up, ffn_conv_w, ffn_conv_b, ffn_w_down, ln2_g, ln2_b):
    _, s, dm = x.shape
    dk = gla_gate_w.shape[2]
    dv = w_proj_b.shape[1]
    n_groups = len(DILATIONS)
    a_cols = 3 * A_WIDTH * n_groups
    gla_cols = 2 * dk + 2 * dv
    low0 = a_cols + gla_cols
    biases = [_band_bias(rel_bias[:, g * A_HEADS:(g + 1) * A_HEADS], d) for g, d in enumerate(DILATIONS)]

    xf = x.reshape(s, dm)
    xb = xf.astype(BF16)
    for i in range(DEPTH):
        wi = w_in[i]
        w_low = jnp.pad(wi[:, low0:low0 + G_RANK], ((0, 0), (0, LANES - G_RANK))).astype(BF16)
        gate_w = jnp.pad(gla_gate_w[i], ((0, LANES - G_RANK), (0, 0)))
        outs, lses = [], []
        for g, d in enumerate(DILATIONS):
            wq = wi[:, 3 * A_WIDTH * g:3 * A_WIDTH * (g + 1)].astype(BF16)
            qkv = _proj(xb, wq, d, BF16)
            o, lse = _attention(qkv, biases[g])
            outs.append(o)
            lses.append(lse)
        ya = _combine(outs, lses, w_proj_a[i].astype(BF16))
        gla_in = _proj(xb, wi[:, a_cols:low0].astype(BF16), 1, F32)
        cum = _gate(xb, w_low, gate_w, gla_gate_b[i].reshape(1, dk))
        a = _gla(gla_in, cum, gla_norm_g[i].reshape(1, -1))
        gates = _proj(xb, wi[:, low0 + G_RANK:].astype(BF16), 1, BF16, act="sigmoid")
        y = _merge(a, w_proj_b[i].astype(BF16), ya, gates)
        xf, xb = _out_ln(y, w_out[i].astype(BF16), xf, ln1_g[i].reshape(1, dm), ln1_b[i].reshape(1, dm))
        h = _ffn1(xb, ffn_w_gate[i].astype(BF16), ffn_w_up[i].astype(BF16),
                  ffn_conv_w[i], ffn_conv_b[i].reshape(1, -1))
        xf, xb = _ffn2(h, ffn_w_down[i].astype(BF16), xf, ln2_g[i].reshape(1, dm), ln2_b[i].reshape(1, dm))
    return xf.reshape(x.shape)
```

```python
import functools
import math

import jax
import jax.numpy as jnp
from jax import lax
from jax.experimental import pallas as pl
from jax.experimental.pallas import tpu as pltpu

F32 = jnp.float32
BF16 = jnp.bfloat16

DEPTH = 4
DILATIONS = (1, 4, 16)
A_HEADS = 8
A_HEAD_DIM = 128
A_WIDTH = A_HEADS * A_HEAD_DIM
A_SPAN = 128
A_BLOCK = 128
A_HEADS_PER_STEP = 2
G_HEADS = 4
G_RANK = 16
G_TAU = 16.0
G_CHUNK = 64
N_BUCKETS = 32
MAX_DIST = 2048
CONV_W = 3
LN_EPS = 1e-5
ALPHA = (2 * DEPTH) ** 0.25
NEG = -1e30
LANES = 128
VMEM_LIMIT = 56 << 20


def _params(*sem):
    return pltpu.CompilerParams(dimension_semantics=sem, vmem_limit_bytes=VMEM_LIMIT)


def _resident(shape, index_map):
    return pl.BlockSpec(shape, index_map, pipeline_mode=pl.Buffered(1))


def _nt(a, b):
    return lax.dot_general(a, b, (((1,), (1,)), ((), ())), preferred_element_type=F32)


def _tn(a, b):
    return lax.dot_general(a, b, (((0,), (0,)), ((), ())), preferred_element_type=F32)


def _dot(a, b):
    return jnp.dot(a, b, preferred_element_type=F32)


def _sigmoid(x):
    return 1.0 / (1.0 + jnp.exp(-x))


def _split2(a):
    hi = a.astype(BF16)
    lo = (a - hi.astype(F32)).astype(BF16)
    return hi, lo


def _split3(a):
    hi = a.astype(BF16)
    r = a - hi.astype(F32)
    mid = r.astype(BF16)
    lo = (r - mid.astype(F32)).astype(BF16)
    return hi, mid, lo


def _layernorm(z, g, b):
    mu = jnp.mean(z, axis=-1, keepdims=True)
    zc = z - mu
    var = jnp.mean(zc * zc, axis=-1, keepdims=True)
    return zc * lax.rsqrt(var + LN_EPS) * g + b


def _rows(start, size, stride):
    return pl.ds(start, size) if stride == 1 else pl.ds(start, size, stride=stride)


def _proj_kernel(x_ref, w_ref, o_ref, *scratch, d, act, cast_w):
    scratch = list(scratch)
    if cast_w:
        wb_ref = scratch.pop(0)

        @pl.when(pl.program_id(1) == 0)
        def _():
            wb_ref[...] = w_ref[0].astype(BF16)

        w = wb_ref[...]
    else:
        w = w_ref[...]
    acc = _dot(x_ref[...], w)
    if act == "sigmoid":
        acc = _sigmoid(acc)
    if d == 1:
        o_ref[0] = acc.astype(o_ref.dtype)
    else:
        acc_ref = scratch.pop(0)
        n = acc.shape[0] // d
        for c in range(acc.shape[1] // LANES):
            cs = slice(c * LANES, (c + 1) * LANES)
            acc_ref[c] = acc[:, cs]
            for r in range(d):
                o_ref[r, :, cs] = acc_ref[c, pl.ds(r, n, stride=d), :].astype(o_ref.dtype)


def _proj(xb, w, d, out_dtype, *, layer=None, col0=0, ncols=None, act=None, tm=1024, tn=1024):
    s, dm = xb.shape
    cast_w = layer is not None
    n = ncols if cast_w else w.shape[1]
    tm, tn = min(tm, s), min(tn, n)
    c0 = col0 // tn
    assert col0 == c0 * tn and n % tn == 0 and s % tm == 0 and tm % (8 * d) == 0
    if cast_w:
        w_spec = pl.BlockSpec((1, dm, tn), lambda j, m: (layer, 0, c0 + j))
    else:
        w_spec = pl.BlockSpec((dm, tn), lambda j, m: (0, j))
    scratch = ([pltpu.VMEM((dm, tn), BF16)] if cast_w else []) + ([pltpu.VMEM((tn // LANES, tm, LANES), F32)] if d > 1 else [])
    return pl.pallas_call(
        functools.partial(_proj_kernel, d=d, act=act, cast_w=cast_w),
        grid=(n // tn, s // tm),
        in_specs=[pl.BlockSpec((tm, dm), lambda j, m: (m, 0)), w_spec],
        out_specs=pl.BlockSpec((d, tm // d, tn), lambda j, m: (0, m, j)),
        out_shape=jax.ShapeDtypeStruct((d, s // d, n), out_dtype),
        scratch_shapes=scratch,
        compiler_params=_params("arbitrary", "arbitrary"),
        name=f"proj_d{d}_{n}",
    )(xb, w)


def _gate_kernel(x_ref, wl_ref, gw_ref, gb_ref, cum_ref, *, n_chunks):
    g = _dot(x_ref[...], wl_ref[0].astype(BF16))
    g_hi, g_lo = _split2(g)
    w_hi, w_lo = _split2(gw_ref[...])
    z = _dot(g_hi, w_hi) + _dot(g_hi, w_lo) + _dot(g_lo, w_hi) + gb_ref[...]
    log_a = (jnp.minimum(z, 0.0) - jnp.log(1.0 + jnp.exp(-jnp.abs(z)))) * (1.0 / G_TAU)
    row = lax.broadcasted_iota(jnp.int32, (G_CHUNK, G_CHUNK), 0)
    col = lax.broadcasted_iota(jnp.int32, (G_CHUNK, G_CHUNK), 1)
    tril = jnp.where(row >= col, 1.0, 0.0).astype(BF16)
    for c in range(n_chunks):
        hi, mid, lo = _split3(log_a[c * G_CHUNK:(c + 1) * G_CHUNK])
        cum_ref[c * G_CHUNK:(c + 1) * G_CHUNK, :] = _dot(tril, hi) + _dot(tril, mid) + _dot(tril, lo)


def _gate(xb, w_in, layer, low0, gate_w, gate_b, tm=512):
    s, dm = xb.shape
    dk = gate_w.shape[1]
    tm = min(tm, s)
    assert low0 % LANES == 0
    return pl.pallas_call(
        functools.partial(_gate_kernel, n_chunks=tm // G_CHUNK),
        grid=(s // tm,),
        in_specs=[pl.BlockSpec((tm, dm), lambda m: (m, 0)),
                  pl.BlockSpec((1, dm, LANES), lambda m: (layer, 0, low0 // LANES)),
                  pl.BlockSpec((LANES, dk), lambda m: (0, 0)),
                  pl.BlockSpec((1, dk), lambda m: (0, 0))],
        out_specs=pl.BlockSpec((tm, dk), lambda m: (m, 0)),
        out_shape=jax.ShapeDtypeStruct((s, dk), F32),
        compiler_params=_params("parallel"),
        name="gla_gate",
    )(xb, w_in, gate_w, gate_b)


def _gla_kernel(q_ref, k_ref, v_ref, c_ref, r_ref, ng_ref, o_ref, st_ref, *, n_chunks, scale):
    @pl.when(pl.program_id(1) == 0)
    def _():
        st_ref[...] = jnp.zeros_like(st_ref)

    row = lax.broadcasted_iota(jnp.int32, (G_CHUNK, G_CHUNK), 0)
    col = lax.broadcasted_iota(jnp.int32, (G_CHUNK, G_CHUNK), 1)
    causal = row >= col
    ng = ng_ref[...]
    for c in range(n_chunks):
        sl = slice(c * G_CHUNK, (c + 1) * G_CHUNK)
        cum = c_ref[sl, :]
        last = cum[G_CHUNK - 1:G_CHUNK, :]
        q = q_ref[0, sl, :].astype(F32) * scale
        k = k_ref[0, sl, :].astype(F32)
        v = v_ref[0, sl, :].astype(BF16)
        qd = (q * jnp.exp(cum)).astype(BF16)
        ki = (k * jnp.exp(-cum)).astype(BF16)
        ke = (k * jnp.exp(last - cum)).astype(BF16)
        att = jnp.where(causal, _nt(qd, ki), 0.0)
        st = st_ref[...]
        o = _dot(att.astype(BF16), v) + _nt(qd, st.astype(BF16))
        st_ref[...] = st * jnp.exp(last) + _tn(v, ke)
        ms = jnp.mean(o * o, axis=-1, keepdims=True)
        on = o * lax.rsqrt(ms + LN_EPS) * ng
        gr = r_ref[0, sl, :].astype(F32)
        o_ref[sl, :] = (on * (gr * _sigmoid(gr))).astype(o_ref.dtype)


def _gla(gla_in, cum, norm_g, tg=512):
    s, dk = cum.shape
    dv = (gla_in.shape[2] - 2 * dk) // 2
    hk, hv = dk // G_HEADS, dv // G_HEADS
    tg = min(tg, s)
    kq = dk // hk
    kv = 2 * dk // hv
    kr = (2 * dk + dv) // hv
    return pl.pallas_call(
        functools.partial(_gla_kernel, n_chunks=tg // G_CHUNK, scale=hk ** -0.5),
        grid=(G_HEADS, s // tg),
        in_specs=[pl.BlockSpec((1, tg, hk), lambda h, j: (0, j, h)),
                  pl.BlockSpec((1, tg, hk), lambda h, j: (0, j, kq + h)),
                  pl.BlockSpec((1, tg, hv), lambda h, j: (0, j, kv + h)),
                  pl.BlockSpec((tg, hk), lambda h, j: (j, h)),
                  pl.BlockSpec((1, tg, hv), lambda h, j: (0, j, kr + h)),
                  pl.BlockSpec((1, hv), lambda h, j: (0, 0))],
        out_specs=pl.BlockSpec((tg, hv), lambda h, j: (j, h)),
        out_shape=jax.ShapeDtypeStruct((s, dv), BF16),
        scratch_shapes=[pltpu.VMEM((hv, hk), F32)],
        compiler_params=_params("parallel", "arbitrary"),
        name="gla",
    )(gla_in, gla_in, gla_in, cum, gla_in, norm_g)


def _attn_kernel(q_ref, kc_ref, vc_ref, kp_ref, vp_ref, b_ref, o_ref, lse_ref, *, d, nblk, scale):
    first = pl.program_id(0) == 0
    hp = pl.program_id(1)

    @pl.when(hp == 0)
    def _():
        lse_ref[...] = jnp.zeros_like(lse_ref)

    lane = lax.broadcasted_iota(jnp.int32, (A_BLOCK, LANES), 1)
    col = lax.broadcasted_iota(jnp.int32, (A_BLOCK, 2 * A_BLOCK), 1)
    for r in range(d):
        for b in range(nblk):
            rows = slice(b * A_BLOCK, (b + 1) * A_BLOCK)
            prev = slice((b - 1) * A_BLOCK, b * A_BLOCK)
            pos = _rows(b * A_BLOCK * d + r, A_BLOCK, d)
            lse_tile = lse_ref[pos, :]
            for hh in range(A_HEADS_PER_STEP):
                head = hp * A_HEADS_PER_STEP + hh
                cs = slice(hh * A_HEAD_DIM, (hh + 1) * A_HEAD_DIM)
                q = q_ref[r, rows, cs]
                if b == 0:
                    kp, vp = kp_ref[r, :, cs], vp_ref[r, :, cs]
                else:
                    kp, vp = kc_ref[r, prev, cs], vc_ref[r, prev, cs]
                kk = jnp.concatenate([kp, kc_ref[r, rows, cs]], axis=0)
                vv = jnp.concatenate([vp, vc_ref[r, rows, cs]], axis=0)
                sc = _nt(q, kk) * scale + b_ref[head]
                if b == 0:
                    sc = jnp.where(jnp.logical_and(first, col < A_BLOCK), NEG, sc)
                m = jnp.max(sc, axis=-1, keepdims=True)
                p = jnp.exp(sc - m)
                l = jnp.sum(p, axis=-1, keepdims=True)
                o_ref[hh, pos, :] = _dot(p.astype(BF16), vv) / l
                lse_tile = jnp.where(lane == head, m + jnp.log(l), lse_tile)
            lse_ref[pos, :] = lse_tile


def _attention(qkv, bias, group, positions_per_step=1024):
    d, n_sub, _ = qkv.shape
    s = d * n_sub
    nblk = max(1, min(positions_per_step // (d * A_BLOCK), n_sub // A_BLOCK))
    tq = nblk * A_BLOCK
    wc = A_HEADS_PER_STEP * A_HEAD_DIM
    nh = A_WIDTH // wc
    cur = lambda c: pl.BlockSpec((d, tq, wc), lambda j, h: (0, j, c * nh + h))
    prv = lambda c: pl.BlockSpec((d, A_BLOCK, wc),
                                 lambda j, h: (0, jnp.maximum(j * nblk - 1, 0), c * nh + h))
    return pl.pallas_call(
        functools.partial(_attn_kernel, d=d, nblk=nblk, scale=A_HEAD_DIM ** -0.5),
        grid=(n_sub // tq, nh),
        in_specs=[cur(0), cur(1), cur(2), prv(1), prv(2),
                  _resident((A_HEADS, A_BLOCK, 2 * A_BLOCK), lambda j, h: (group, 0, 0))],
        out_specs=[pl.BlockSpec((A_HEADS_PER_STEP, d * tq, A_HEAD_DIM), lambda j, h: (h, j, 0)),
                   pl.BlockSpec((d * tq, LANES), lambda j, h: (j, 0))],
        out_shape=[jax.ShapeDtypeStruct((A_HEADS, s, A_HEAD_DIM), F32),
                   jax.ShapeDtypeStruct((s, LANES), F32)],
        compiler_params=_params("parallel", "arbitrary"),
        name=f"attn_d{d}",
    )(qkv, qkv, qkv, qkv, qkv, bias)


def _combine_kernel(o0_ref, o1_ref, o2_ref, l0_ref, l1_ref, l2_ref, w_ref, y_ref):
    l0, l1, l2 = l0_ref[...], l1_ref[...], l2_ref[...]
    m = jnp.maximum(jnp.maximum(l0, l1), l2)
    e0, e1, e2 = jnp.exp(l0 - m), jnp.exp(l1 - m), jnp.exp(l2 - m)
    inv = 1.0 / (e0 + e1 + e2)
    w0, w1, w2 = e0 * inv, e1 * inv, e2 * inv
    pieces = []
    for h in range(A_HEADS):
        ya = w0[:, h:h + 1] * o0_ref[h] + w1[:, h:h + 1] * o1_ref[h] + w2[:, h:h + 1] * o2_ref[h]
        pieces.append(ya.astype(BF16))
    y_ref[...] = _dot(jnp.concatenate(pieces, axis=1), w_ref[...]).astype(y_ref.dtype)


def _combine(outs, lses, w_proj_a, tm=512):
    s = outs[0].shape[1]
    dm = w_proj_a.shape[1]
    tm = min(tm, s)
    o_spec = pl.BlockSpec((A_HEADS, tm, A_HEAD_DIM), lambda m: (0, m, 0))
    l_spec = pl.BlockSpec((tm, LANES), lambda m: (m, 0))
    return pl.pallas_call(
        _combine_kernel,
        grid=(s // tm,),
        in_specs=[o_spec, o_spec, o_spec, l_spec, l_spec, l_spec,
                  _resident((A_WIDTH, dm), lambda m: (0, 0))],
        out_specs=pl.BlockSpec((tm, dm), lambda m: (m, 0)),
        out_shape=jax.ShapeDtypeStruct((s, dm), F32),
        compiler_params=_params("parallel"),
        name="combine_proj_a",
    )(*outs, *lses, w_proj_a)


def _merge_kernel(a_ref, w_ref, ya_ref, sa_ref, sb_ref, y_ref):
    yb = _dot(a_ref[...], w_ref[...])
    y = sa_ref[0].astype(F32) * ya_ref[...] + sb_ref[0].astype(F32) * yb
    y_ref[...] = y.astype(y_ref.dtype)


def _merge(a, w_proj_b, ya, gates, tm=1024, tn=1024):
    s, dv = a.shape
    dm = w_proj_b.shape[1]
    tm, tn = min(tm, s), min(tn, dm)
    nb = dm // tn
    return pl.pallas_call(
        _merge_kernel,
        grid=(nb, s // tm),
        in_specs=[pl.BlockSpec((tm, dv), lambda j, m: (m, 0)),
                  pl.BlockSpec((dv, tn), lambda j, m: (0, j)),
                  pl.BlockSpec((tm, tn), lambda j, m: (m, j)),
                  pl.BlockSpec((1, tm, tn), lambda j, m: (0, m, j)),
                  pl.BlockSpec((1, tm, tn), lambda j, m: (0, m, nb + j))],
        out_specs=pl.BlockSpec((tm, tn), lambda j, m: (m, j)),
        out_shape=jax.ShapeDtypeStruct((s, dm), BF16),
        compiler_params=_params("parallel", "parallel"),
        name="proj_b_merge",
    )(a, w_proj_b, ya, gates, gates)


def _out_ln_kernel(y_ref, w_ref, x_ref, g_ref, b_ref, xo_ref, xb_ref):
    z = ALPHA * x_ref[...] + _dot(y_ref[...], w_ref[...])
    xn = _layernorm(z, g_ref[...], b_ref[...])
    xo_ref[...] = xn
    xb_ref[...] = xn.astype(BF16)


def _out_ln(y, w_out, x, g, b, tm=512):
    s, dm = x.shape
    tm = min(tm, s)
    row = pl.BlockSpec((tm, dm), lambda m: (m, 0))
    vec = pl.BlockSpec((1, dm), lambda m: (0, 0))
    return pl.pallas_call(
        _out_ln_kernel,
        grid=(s // tm,),
        in_specs=[row, _resident((dm, dm), lambda m: (0, 0)), row, vec, vec],
        out_specs=[row, row],
        out_shape=[jax.ShapeDtypeStruct((s, dm), F32), jax.ShapeDtypeStruct((s, dm), BF16)],
        compiler_params=_params("parallel"),
        name="out_proj_ln",
    )(y, w_out, x, g, b)


def _ffn1_kernel(x_ref, wg_ref, wu_ref, cw_ref, cb_ref, h_ref, wgb_ref, wub_ref, halo_ref):
    tm = x_ref.shape[0]

    @pl.when(pl.program_id(1) == 0)
    def _():
        wgb_ref[...] = wg_ref[0].astype(BF16)
        wub_ref[...] = wu_ref[0].astype(BF16)
        halo_ref[...] = jnp.zeros_like(halo_ref)

    x = x_ref[...]
    g = _dot(x, wgb_ref[...])
    u = _dot(x, wub_ref[...])
    halo = halo_ref[...]
    row = lax.broadcasted_iota(jnp.int32, g.shape, 0)
    g1 = jnp.where(row == 0, halo[7:8], pltpu.roll(g, 1, axis=0))
    g2 = jnp.where(row == 0, halo[6:7], jnp.where(row == 1, halo[7:8], pltpu.roll(g, 2, axis=0)))
    halo_ref[...] = g[tm - 8:tm]
    cw = cw_ref[0]
    gc = cb_ref[0] + cw[0:1] * g2 + cw[1:2] * g1 + cw[2:3] * g
    h_ref[...] = (gc * _sigmoid(gc) * u).astype(h_ref.dtype)


def _ffn1(xb, w_gate, w_up, conv_w, conv_b, layer, tm=1024, tn=512):
    s, dm = xb.shape
    dff = w_gate.shape[2]
    tm = min(tm, s)
    wspec = pl.BlockSpec((1, dm, tn), lambda j, m: (layer, 0, j))
    return pl.pallas_call(
        _ffn1_kernel,
        grid=(dff // tn, s // tm),
        in_specs=[pl.BlockSpec((tm, dm), lambda j, m: (m, 0)), wspec, wspec,
                  pl.BlockSpec((1, CONV_W, tn), lambda j, m: (layer, 0, j)),
                  pl.BlockSpec((1, 1, tn), lambda j, m: (layer, 0, j))],
        out_specs=pl.BlockSpec((tm, tn), lambda j, m: (m, j)),
        out_shape=jax.ShapeDtypeStruct((s, dff), BF16),
        scratch_shapes=[pltpu.VMEM((dm, tn), BF16), pltpu.VMEM((dm, tn), BF16),
                        pltpu.VMEM((8, tn), F32)],
        compiler_params=_params("arbitrary", "arbitrary"),
        name="ffn_gate_up_conv",
    )(xb, w_gate, w_up, conv_w, conv_b)


def _ffn2_kernel(h_ref, w_ref, x_ref, g_ref, b_ref, xo_ref, xb_ref):
    k = pl.program_id(1)

    @pl.when(k == 0)
    def _():
        xo_ref[...] = ALPHA * x_ref[...]

    slab = 512
    for c in range(xo_ref.shape[1] // slab):
        cs = slice(c * slab, (c + 1) * slab)
        xo_ref[:, cs] += _dot(h_ref[...], w_ref[:, cs])

    @pl.when(k == pl.num_programs(1) - 1)
    def _():
        xn = _layernorm(xo_ref[...], g_ref[...], b_ref[...])
        xo_ref[...] = xn
        xb_ref[...] = xn.astype(BF16)


def _ffn2(h, w_down, x, g, b, tm=1024, tk=512):
    s, dm = x.shape
    dff = h.shape[1]
    tm = min(tm, s)
    row = pl.BlockSpec((tm, dm), lambda m, k: (m, 0))
    vec = pl.BlockSpec((1, dm), lambda m, k: (0, 0))
    return pl.pallas_call(
        _ffn2_kernel,
        grid=(s // tm, dff // tk),
        in_specs=[pl.BlockSpec((tm, tk), lambda m, k: (m, k)),
                  pl.BlockSpec((tk, dm), lambda m, k: (k, 0)), row, vec, vec],
        out_specs=[row, row],
        out_shape=[jax.ShapeDtypeStruct((s, dm), F32), jax.ShapeDtypeStruct((s, dm), BF16)],
        compiler_params=_params("parallel", "arbitrary"),
        name="ffn_down_ln",
    )(h, w_down, x, g, b)


def _band_buckets(dilation):
    qi = jnp.arange(A_BLOCK)[:, None]
    ci = jnp.arange(2 * A_BLOCK)[None, :]
    off = A_BLOCK + qi - ci
    valid = (off >= 0) & (off <= A_SPAN)
    dist = dilation * jnp.clip(off, 0, A_SPAN)
    max_exact = N_BUCKETS // 2
    df = jnp.maximum(dist, 1).astype(F32)
    large = max_exact + (jnp.log(df / max_exact) / math.log(MAX_DIST / max_exact)
                         * (N_BUCKETS - max_exact)).astype(jnp.int32)
    bucket = jnp.where(dist < max_exact, dist, jnp.minimum(large, N_BUCKETS - 1))
    return jnp.where(valid, bucket, -1).astype(jnp.int32)


def _bias_kernel(tab_ref, idx_ref, o_ref):
    g = pl.program_id(0)
    idx = idx_ref[0]
    for h in range(A_HEADS):
        acc = jnp.full(idx.shape, NEG, F32)
        for b in range(N_BUCKETS):
            acc = jnp.where(idx == b, tab_ref[b, g * A_HEADS + h], acc)
        o_ref[h] = acc


def _band_bias(rel_bias):
    idx = jnp.stack([_band_buckets(d) for d in DILATIONS])
    n_groups = len(DILATIONS)
    return pl.pallas_call(
        _bias_kernel,
        grid=(n_groups,),
        in_specs=[pl.BlockSpec(memory_space=pltpu.SMEM),
                  pl.BlockSpec((1, A_BLOCK, 2 * A_BLOCK), lambda g: (g, 0, 0))],
        out_specs=pl.BlockSpec((A_HEADS, A_BLOCK, 2 * A_BLOCK), lambda g: (g, 0, 0)),
        out_shape=jax.ShapeDtypeStruct((n_groups * A_HEADS, A_BLOCK, 2 * A_BLOCK), F32),
        compiler_params=_params("parallel"),
        name="band_bias",
    )(rel_bias, idx)


def kernel(x, w_in, gla_gate_w, gla_gate_b, gla_norm_g, w_proj_a, w_proj_b, w_out, rel_bias,
           ln1_g, ln1_b, ffn_w_gate, ffn_w_up, ffn_conv_w, ffn_conv_b, ffn_w_down, ln2_g, ln2_b):
    _, s, dm = x.shape
    dk = gla_gate_w.shape[2]
    dv = w_proj_b.shape[1]
    n_groups = len(DILATIONS)
    a_cols = 3 * A_WIDTH * n_groups
    gla_cols = 2 * dk + 2 * dv
    low0 = a_cols + gla_cols
    bias = _band_bias(rel_bias)

    xf = x.reshape(s, dm)
    xb = xf.astype(BF16)
    for i in range(DEPTH):
        gate_w = jnp.pad(gla_gate_w[i], ((0, LANES - G_RANK), (0, 0)))
        outs, lses = [], []
        for g, d in enumerate(DILATIONS):
            qkv = _proj(xb, w_in, d, BF16, layer=i, col0=3 * A_WIDTH * g, ncols=3 * A_WIDTH)
            o, lse = _attention(qkv, bias, g)
            outs.append(o)
            lses.append(lse)
        ya = _combine(outs, lses, w_proj_a[i].astype(BF16))
        gla_in = _proj(xb, w_in, 1, F32, layer=i, col0=a_cols, ncols=gla_cols)
        cum = _gate(xb, w_in, i, low0, gate_w, gla_gate_b[i].reshape(1, dk))
        a = _gla(gla_in, cum, gla_norm_g[i].reshape(1, -1))
        gates = _proj(xb, w_in[i, :, low0 + G_RANK:].astype(BF16), 1, BF16, act="sigmoid")
        y = _merge(a, w_proj_b[i].astype(BF16), ya, gates)
        xf, xb = _out_ln(y, w_out[i].astype(BF16), xf, ln1_g[i].reshape(1, dm), ln1_b[i].reshape(1, dm))
        h = _ffn1(xb, ffn_w_gate, ffn_w_up, ffn_conv_w, ffn_conv_b.reshape(DEPTH, 1, -1), i)
        xf, xb = _ffn2(h, ffn_w_down[i].astype(BF16), xf, ln2_g[i].reshape(1, dm), ln2_b[i].reshape(1, dm))
    return xf.reshape(x.shape)
```

```python
import functools
import math

import jax
import jax.numpy as jnp
from jax import lax
from jax.experimental import pallas as pl
from jax.experimental.pallas import tpu as pltpu

F32 = jnp.float32
BF16 = jnp.bfloat16

DEPTH = 4
DILATIONS = (1, 4, 16)
A_HEADS = 8
A_HEAD_DIM = 128
A_WIDTH = A_HEADS * A_HEAD_DIM
A_SPAN = 128
A_BLOCK = 128
A_STEP_BLOCKS = 4
G_HEADS = 4
G_RANK = 16
G_TAU = 16.0
G_CHUNK = 64
N_BUCKETS = 32
MAX_DIST = 2048
CONV_W = 3
LN_EPS = 1e-5
ALPHA = (2 * DEPTH) ** 0.25
NEG = -1e30
LANES = 128
SUBLANES = 8
VMEM_LIMIT = 56 << 20


def _params(*sem):
    return pltpu.CompilerParams(dimension_semantics=sem, vmem_limit_bytes=VMEM_LIMIT)


def _resident(shape, index_map):
    return pl.BlockSpec(shape, index_map, pipeline_mode=pl.Buffered(1))


def _nt(a, b):
    return lax.dot_general(a, b, (((1,), (1,)), ((), ())), preferred_element_type=F32)


def _tn(a, b):
    return lax.dot_general(a, b, (((0,), (0,)), ((), ())), preferred_element_type=F32)


def _dot(a, b):
    return jnp.dot(a, b, preferred_element_type=F32)


def _sigmoid(x):
    return 1.0 / (1.0 + jnp.exp(-x))


def _split2(a):
    hi = a.astype(BF16)
    lo = (a - hi.astype(F32)).astype(BF16)
    return hi, lo


def _layernorm(z, g, b):
    mu = jnp.mean(z, axis=-1, keepdims=True)
    zc = z - mu
    var = jnp.mean(zc * zc, axis=-1, keepdims=True)
    return zc * lax.rsqrt(var + LN_EPS) * g + b


def _rows(start, size, stride):
    return pl.ds(start, size) if stride == 1 else pl.ds(start, size, stride=stride)


def _proj_kernel(*refs, d, act, shift):
    refs = list(refs)
    x_ref, w_ref = refs.pop(0), refs.pop(0)
    wn_ref = refs.pop(0) if shift else None
    o_ref, wb_ref = refs.pop(0), refs.pop(0)

    @pl.when(pl.program_id(1) == 0)
    def _():
        w = w_ref[0]
        if shift:
            tn = w.shape[0]
            w = jnp.concatenate([w, wn_ref[0]], axis=0)[shift:shift + tn]
        wb_ref[...] = w.astype(BF16)

    acc = _nt(x_ref[...], wb_ref[...])
    if act == "sigmoid":
        acc = _sigmoid(acc)
    if d == 1:
        o_ref[0] = acc.astype(o_ref.dtype)
    else:
        acc_ref = refs.pop(0)
        n = acc.shape[0] // d
        for c in range(acc.shape[1] // LANES):
            cs = slice(c * LANES, (c + 1) * LANES)
            acc_ref[c] = acc[:, cs]
            for r in range(d):
                o_ref[r, :, cs] = acc_ref[c, pl.ds(r, n, stride=d), :].astype(o_ref.dtype)


def _proj(xb, w_t, layer, col0, ncols, d, out_dtype, *, act=None, tm=1024, tn=1024):
    s, dm = xb.shape
    tm, tn = min(tm, s), min(tn, ncols)
    shift = col0 % LANES
    c0 = (col0 - shift) // tn
    assert col0 - shift == c0 * tn and ncols % tn == 0 and s % tm == 0 and tm % (8 * d) == 0
    assert shift % SUBLANES == 0
    in_specs = [pl.BlockSpec((tm, dm), lambda j, m: (m, 0)),
                pl.BlockSpec((1, tn, dm), lambda j, m: (layer, c0 + j, 0))]
    operands = [xb, w_t]
    if shift:
        per = tn // LANES
        in_specs.append(pl.BlockSpec((1, LANES, dm), lambda j, m: (layer, (c0 + j + 1) * per, 0)))
        operands.append(w_t)
    scratch = [pltpu.VMEM((tn, dm), BF16)] + ([pltpu.VMEM((tn // LANES, tm, LANES), F32)] if d > 1 else [])
    return pl.pallas_call(
        functools.partial(_proj_kernel, d=d, act=act, shift=shift),
        grid=(ncols // tn, s // tm),
        in_specs=in_specs,
        out_specs=pl.BlockSpec((d, tm // d, tn), lambda j, m: (0, m, j)),
        out_shape=jax.ShapeDtypeStruct((d, s // d, ncols), out_dtype),
        scratch_shapes=scratch,
        compiler_params=_params("arbitrary", "arbitrary"),
        name=f"proj_d{d}_{ncols}",
    )(*operands)


def _gate_kernel(x_ref, wl_ref, gw_ref, gb_ref, cum_ref, *, n_chunks):
    g = _nt(x_ref[...], wl_ref[0].astype(BF16))
    g_hi, g_lo = _split2(g)
    w_hi, w_lo = _split2(gw_ref[...])
    z = _dot(g_hi, w_hi) + _dot(g_hi, w_lo) + _dot(g_lo, w_hi) + gb_ref[...]
    log_a = (jnp.minimum(z, 0.0) - jnp.log(1.0 + jnp.exp(-jnp.abs(z)))) * (1.0 / G_TAU)
    row = lax.broadcasted_iota(jnp.int32, (G_CHUNK, G_CHUNK), 0)
    col = lax.broadcasted_iota(jnp.int32, (G_CHUNK, G_CHUNK), 1)
    tril = jnp.where(row >= col, 1.0, 0.0).astype(BF16)
    for c in range(n_chunks):
        hi, lo = _split2(log_a[c * G_CHUNK:(c + 1) * G_CHUNK])
        cum_ref[c * G_CHUNK:(c + 1) * G_CHUNK, :] = _dot(tril, hi) + _dot(tril, lo)


def _gate(xb, w_t, layer, low0, gate_w, gate_b, tm=512):
    s, dm = xb.shape
    dk = gate_w.shape[1]
    tm = min(tm, s)
    assert low0 % LANES == 0
    return pl.pallas_call(
        functools.partial(_gate_kernel, n_chunks=tm // G_CHUNK),
        grid=(s // tm,),
        in_specs=[pl.BlockSpec((tm, dm), lambda m: (m, 0)),
                  pl.BlockSpec((1, LANES, dm), lambda m: (layer, low0 // LANES, 0)),
                  pl.BlockSpec((LANES, dk), lambda m: (0, 0)),
                  pl.BlockSpec((1, dk), lambda m: (0, 0))],
        out_specs=pl.BlockSpec((tm, dk), lambda m: (m, 0)),
        out_shape=jax.ShapeDtypeStruct((s, dk), F32),
        compiler_params=_params("parallel"),
        name="gla_gate",
    )(xb, w_t, gate_w, gate_b)


def _gla_kernel(q_ref, k_ref, v_ref, c_ref, r_ref, ng_ref, o_ref, st_ref, *, n_chunks, scale):
    @pl.when(pl.program_id(1) == 0)
    def _():
        st_ref[...] = jnp.zeros_like(st_ref)

    row = lax.broadcasted_iota(jnp.int32, (G_CHUNK, G_CHUNK), 0)
    col = lax.broadcasted_iota(jnp.int32, (G_CHUNK, G_CHUNK), 1)
    causal = row >= col
    ng = ng_ref[...]
    for c in range(n_chunks):
        sl = slice(c * G_CHUNK, (c + 1) * G_CHUNK)
        cum = c_ref[sl, :]
        last = cum[G_CHUNK - 1:G_CHUNK, :]
        q = q_ref[0, sl, :].astype(F32) * scale
        k = k_ref[0, sl, :].astype(F32)
        v = v_ref[0, sl, :].astype(BF16)
        qd = (q * jnp.exp(cum)).astype(BF16)
        ki = (k * jnp.exp(-cum)).astype(BF16)
        ke = (k * jnp.exp(last - cum)).astype(BF16)
        att = jnp.where(causal, _nt(qd, ki), 0.0)
        st = st_ref[...]
        o = _dot(att.astype(BF16), v) + _nt(qd, st.astype(BF16))
        st_ref[...] = st * jnp.exp(last) + _tn(v, ke)
        ms = jnp.mean(o * o, axis=-1, keepdims=True)
        on = o * lax.rsqrt(ms + LN_EPS) * ng
        gr = r_ref[0, sl, :].astype(F32)
        o_ref[sl, :] = (on * (gr * _sigmoid(gr))).astype(o_ref.dtype)


def _gla(gla_in, cum, norm_g, tg=512):
    s, dk = cum.shape
    dv = (gla_in.shape[2] - 2 * dk) // 2
    hk, hv = dk // G_HEADS, dv // G_HEADS
    tg = min(tg, s)
    kq = dk // hk
    kv = 2 * dk // hv
    kr = (2 * dk + dv) // hv
    return pl.pallas_call(
        functools.partial(_gla_kernel, n_chunks=tg // G_CHUNK, scale=hk ** -0.5),
        grid=(G_HEADS, s // tg),
        in_specs=[pl.BlockSpec((1, tg, hk), lambda h, j: (0, j, h)),
                  pl.BlockSpec((1, tg, hk), lambda h, j: (0, j, kq + h)),
                  pl.BlockSpec((1, tg, hv), lambda h, j: (0, j, kv + h)),
                  pl.BlockSpec((tg, hk), lambda h, j: (j, h)),
                  pl.BlockSpec((1, tg, hv), lambda h, j: (0, j, kr + h)),
                  pl.BlockSpec((1, hv), lambda h, j: (0, 0))],
        out_specs=pl.BlockSpec((tg, hv), lambda h, j: (j, h)),
        out_shape=jax.ShapeDtypeStruct((s, dv), BF16),
        scratch_shapes=[pltpu.VMEM((hv, hk), F32)],
        compiler_params=_params("parallel", "arbitrary"),
        name="gla",
    )(gla_in, gla_in, gla_in, cum, gla_in, norm_g)


def _attn_kernel(q_ref, kc_ref, vc_ref, kp_ref, vp_ref, b_ref, o_ref, lse_ref, *, d, rpg, nblk, scale):
    first = pl.program_id(0) == 0
    r0 = pl.program_id(1) * rpg if rpg < d else 0
    lane = lax.broadcasted_iota(jnp.int32, (A_BLOCK, LANES), 1)
    col = lax.broadcasted_iota(jnp.int32, (A_BLOCK, 2 * A_BLOCK), 1)
    for rl in range(rpg):
        for b in range(nblk):
            rows = slice(b * A_BLOCK, (b + 1) * A_BLOCK)
            prev = slice((b - 1) * A_BLOCK, b * A_BLOCK)
            pos = _rows(b * A_BLOCK * d + r0 + rl, A_BLOCK, d)
            lse_tile = jnp.zeros((A_BLOCK, LANES), F32)
            for h in range(A_HEADS):
                cs = slice(h * A_HEAD_DIM, (h + 1) * A_HEAD_DIM)
                q = q_ref[rl, rows, cs]
                if b == 0:
                    kp, vp = kp_ref[rl, :, cs], vp_ref[rl, :, cs]
                else:
                    kp, vp = kc_ref[rl, prev, cs], vc_ref[rl, prev, cs]
                kk = jnp.concatenate([kp, kc_ref[rl, rows, cs]], axis=0)
                vv = jnp.concatenate([vp, vc_ref[rl, rows, cs]], axis=0)
                sc = _nt(q, kk) * scale + b_ref[h]
                if b == 0:
                    sc = jnp.where(jnp.logical_and(first, col < A_BLOCK), NEG, sc)
                m = jnp.max(sc, axis=-1, keepdims=True)
                p = jnp.exp(sc - m)
                l = jnp.sum(p, axis=-1, keepdims=True)
                o_ref[h, pos, :] = _dot(p.astype(BF16), vv) / l
                lse_tile = jnp.where(lane == h, m + jnp.log(l), lse_tile)
            lse_ref[pos, :] = lse_tile


def _attention(qkv, bias, group):
    d, n_sub, _ = qkv.shape
    s = d * n_sub
    rpg = min(d, A_STEP_BLOCKS)
    nblk = max(1, min(A_STEP_BLOCKS // rpg, n_sub // A_BLOCK))
    tq = nblk * A_BLOCK
    cur = lambda c: pl.BlockSpec((rpg, tq, A_WIDTH), lambda j, g: (g, j, c))
    prv = lambda c: pl.BlockSpec((rpg, A_BLOCK, A_WIDTH),
                                 lambda j, g: (g, jnp.maximum(j * nblk - 1, 0), c))
    return pl.pallas_call(
        functools.partial(_attn_kernel, d=d, rpg=rpg, nblk=nblk, scale=A_HEAD_DIM ** -0.5),
        grid=(n_sub // tq, d // rpg),
        in_specs=[cur(0), cur(1), cur(2), prv(1), prv(2),
                  _resident((A_HEADS, A_BLOCK, 2 * A_BLOCK), lambda j, g: (group, 0, 0))],
        out_specs=[pl.BlockSpec((A_HEADS, d * tq, A_HEAD_DIM), lambda j, g: (0, j, 0)),
                   pl.BlockSpec((d * tq, LANES), lambda j, g: (j, 0))],
        out_shape=[jax.ShapeDtypeStruct((A_HEADS, s, A_HEAD_DIM), F32),
                   jax.ShapeDtypeStruct((s, LANES), F32)],
        compiler_params=_params("parallel", "arbitrary"),
        name=f"attn_d{d}",
    )(qkv, qkv, qkv, qkv, qkv, bias)


def _combine_kernel(o0_ref, o1_ref, o2_ref, l0_ref, l1_ref, l2_ref, w_ref, y_ref):
    l0, l1, l2 = l0_ref[...], l1_ref[...], l2_ref[...]
    m = jnp.maximum(jnp.maximum(l0, l1), l2)
    e0, e1, e2 = jnp.exp(l0 - m), jnp.exp(l1 - m), jnp.exp(l2 - m)
    inv = 1.0 / (e0 + e1 + e2)
    w0, w1, w2 = e0 * inv, e1 * inv, e2 * inv
    pieces = []
    for h in range(A_HEADS):
        ya = w0[:, h:h + 1] * o0_ref[h] + w1[:, h:h + 1] * o1_ref[h] + w2[:, h:h + 1] * o2_ref[h]
        pieces.append(ya.astype(BF16))
    y_ref[...] = _dot(jnp.concatenate(pieces, axis=1), w_ref[...]).astype(y_ref.dtype)


def _combine(outs, lses, w_proj_a, tm=512):
    s = outs[0].shape[1]
    dm = w_proj_a.shape[1]
    tm = min(tm, s)
    o_spec = pl.BlockSpec((A_HEADS, tm, A_HEAD_DIM), lambda m: (0, m, 0))
    l_spec = pl.BlockSpec((tm, LANES), lambda m: (m, 0))
    return pl.pallas_call(
        _combine_kernel,
        grid=(s // tm,),
        in_specs=[o_spec, o_spec, o_spec, l_spec, l_spec, l_spec,
                  _resident((A_WIDTH, dm), lambda m: (0, 0))],
        out_specs=pl.BlockSpec((tm, dm), lambda m: (m, 0)),
        out_shape=jax.ShapeDtypeStruct((s, dm), F32),
        compiler_params=_params("parallel"),
        name="combine_proj_a",
    )(*outs, *lses, w_proj_a)


def _merge_kernel(a_ref, w_ref, ya_ref, sa_ref, sb_ref, y_ref):
    yb = _dot(a_ref[...], w_ref[...])
    y = sa_ref[0].astype(F32) * ya_ref[...] + sb_ref[0].astype(F32) * yb
    y_ref[...] = y.astype(y_ref.dtype)


def _merge(a, w_proj_b, ya, gates, tm=1024, tn=1024):
    s, dv = a.shape
    dm = w_proj_b.shape[1]
    tm, tn = min(tm, s), min(tn, dm)
    nb = dm // tn
    return pl.pallas_call(
        _merge_kernel,
        grid=(nb, s // tm),
        in_specs=[pl.BlockSpec((tm, dv), lambda j, m: (m, 0)),
                  pl.BlockSpec((dv, tn), lambda j, m: (0, j)),
                  pl.BlockSpec((tm, tn), lambda j, m: (m, j)),
                  pl.BlockSpec((1, tm, tn), lambda j, m: (0, m, j)),
                  pl.BlockSpec((1, tm, tn), lambda j, m: (0, m, nb + j))],
        out_specs=pl.BlockSpec((tm, tn), lambda j, m: (m, j)),
        out_shape=jax.ShapeDtypeStruct((s, dm), BF16),
        compiler_params=_params("parallel", "parallel"),
        name="proj_b_merge",
    )(a, w_proj_b, ya, gates, gates)


def _out_ln_kernel(y_ref, w_ref, x_ref, g_ref, b_ref, xo_ref, xb_ref):
    z = ALPHA * x_ref[...] + _dot(y_ref[...], w_ref[...])
    xn = _layernorm(z, g_ref[...], b_ref[...])
    xo_ref[...] = xn
    xb_ref[...] = xn.astype(BF16)


def _out_ln(y, w_out, x, g, b, tm=512):
    s, dm = x.shape
    tm = min(tm, s)
    row = pl.BlockSpec((tm, dm), lambda m: (m, 0))
    vec = pl.BlockSpec((1, dm), lambda m: (0, 0))
    return pl.pallas_call(
        _out_ln_kernel,
        grid=(s // tm,),
        in_specs=[row, _resident((dm, dm), lambda m: (0, 0)), row, vec, vec],
        out_specs=[row, row],
        out_shape=[jax.ShapeDtypeStruct((s, dm), F32), jax.ShapeDtypeStruct((s, dm), BF16)],
        compiler_params=_params("parallel"),
        name="out_proj_ln",
    )(y, w_out, x, g, b)


def _ffn1_kernel(x_ref, wg_ref, wu_ref, cw_ref, cb_ref, h_ref, wgb_ref, wub_ref, halo_ref):
    tm = x_ref.shape[0]

    @pl.when(pl.program_id(1) == 0)
    def _():
        wgb_ref[...] = wg_ref[0].astype(BF16)
        wub_ref[...] = wu_ref[0].astype(BF16)
        halo_ref[...] = jnp.zeros_like(halo_ref)

    x = x_ref[...]
    g = _dot(x, wgb_ref[...])
    u = _dot(x, wub_ref[...])
    halo = halo_ref[...]
    row = lax.broadcasted_iota(jnp.int32, g.shape, 0)
    g1 = jnp.where(row == 0, halo[7:8], pltpu.roll(g, 1, axis=0))
    g2 = jnp.where(row == 0, halo[6:7], jnp.where(row == 1, halo[7:8], pltpu.roll(g, 2, axis=0)))
    halo_ref[...] = g[tm - 8:tm]
    cw = cw_ref[0]
    gc = cb_ref[0] + cw[0:1] * g2 + cw[1:2] * g1 + cw[2:3] * g
    h_ref[...] = (gc * _sigmoid(gc) * u).astype(h_ref.dtype)


def _ffn1(xb, w_gate, w_up, conv_w, conv_b, layer, tm=1024, tn=512):
    s, dm = xb.shape
    dff = w_gate.shape[2]
    tm = min(tm, s)
    wspec = pl.BlockSpec((1, dm, tn), lambda j, m: (layer, 0, j))
    return pl.pallas_call(
        _ffn1_kernel,
        grid=(dff // tn, s // tm),
        in_specs=[pl.BlockSpec((tm, dm), lambda j, m: (m, 0)), wspec, wspec,
                  pl.BlockSpec((1, CONV_W, tn), lambda j, m: (layer, 0, j)),
                  pl.BlockSpec((1, 1, tn), lambda j, m: (layer, 0, j))],
        out_specs=pl.BlockSpec((tm, tn), lambda j, m: (m, j)),
        out_shape=jax.ShapeDtypeStruct((s, dff), BF16),
        scratch_shapes=[pltpu.VMEM((dm, tn), BF16), pltpu.VMEM((dm, tn), BF16),
                        pltpu.VMEM((8, tn), F32)],
        compiler_params=_params("arbitrary", "arbitrary"),
        name="ffn_gate_up_conv",
    )(xb, w_gate, w_up, conv_w, conv_b)


def _ffn2_kernel(h_ref, w_ref, x_ref, g_ref, b_ref, xo_ref, xb_ref):
    k = pl.program_id(1)

    @pl.when(k == 0)
    def _():
        xo_ref[...] = ALPHA * x_ref[...]

    slab = 512
    for c in range(xo_ref.shape[1] // slab):
        cs = slice(c * slab, (c + 1) * slab)
        xo_ref[:, cs] += _dot(h_ref[...], w_ref[:, cs])

    @pl.when(k == pl.num_programs(1) - 1)
    def _():
        xn = _layernorm(xo_ref[...], g_ref[...], b_ref[...])
        xo_ref[...] = xn
        xb_ref[...] = xn.astype(BF16)


def _ffn2(h, w_down, x, g, b, tm=1024, tk=512):
    s, dm = x.shape
    dff = h.shape[1]
    tm = min(tm, s)
    row = pl.BlockSpec((tm, dm), lambda m, k: (m, 0))
    vec = pl.BlockSpec((1, dm), lambda m, k: (0, 0))
    return pl.pallas_call(
        _ffn2_kernel,
        grid=(s // tm, dff // tk),
        in_specs=[pl.BlockSpec((tm, tk), lambda m, k: (m, k)),
                  pl.BlockSpec((tk, dm), lambda m, k: (k, 0)), row, vec, vec],
        out_specs=[row, row],
        out_shape=[jax.ShapeDtypeStruct((s, dm), F32), jax.ShapeDtypeStruct((s, dm), BF16)],
        compiler_params=_params("parallel", "arbitrary"),
        name="ffn_down_ln",
    )(h, w_down, x, g, b)


def _band_buckets(dilation):
    qi = jnp.arange(A_BLOCK)[:, None]
    ci = jnp.arange(2 * A_BLOCK)[None, :]
    off = A_BLOCK + qi - ci
    valid = (off >= 0) & (off <= A_SPAN)
    dist = dilation * jnp.clip(off, 0, A_SPAN)
    max_exact = N_BUCKETS // 2
    df = jnp.maximum(dist, 1).astype(F32)
    large = max_exact + (jnp.log(df / max_exact) / math.log(MAX_DIST / max_exact)
                         * (N_BUCKETS - max_exact)).astype(jnp.int32)
    bucket = jnp.where(dist < max_exact, dist, jnp.minimum(large, N_BUCKETS - 1))
    return jnp.where(valid, bucket, -1).astype(jnp.int32)


def _bias_kernel(tab_ref, idx_ref, o_ref):
    g = pl.program_id(0)
    idx = idx_ref[0]
    for h in range(A_HEADS):
        acc = jnp.full(idx.shape, NEG, F32)
        for b in range(N_BUCKETS):
            acc = jnp.where(idx == b, tab_ref[b, g * A_HEADS + h], acc)
        o_ref[h] = acc


def _band_bias(rel_bias):
    idx = jnp.stack([_band_buckets(d) for d in DILATIONS])
    n_groups = len(DILATIONS)
    return pl.pallas_call(
        _bias_kernel,
        grid=(n_groups,),
        in_specs=[pl.BlockSpec(memory_space=pltpu.SMEM),
                  pl.BlockSpec((1, A_BLOCK, 2 * A_BLOCK), lambda g: (g, 0, 0))],
        out_specs=pl.BlockSpec((A_HEADS, A_BLOCK, 2 * A_BLOCK), lambda g: (g, 0, 0)),
        out_shape=jax.ShapeDtypeStruct((n_groups * A_HEADS, A_BLOCK, 2 * A_BLOCK), F32),
        compiler_params=_params("parallel"),
        name="band_bias",
    )(rel_bias, idx)


def kernel(x, w_in, gla_gate_w, gla_gate_b, gla_norm_g, w_proj_a, w_proj_b, w_out, rel_bias,
           ln1_g, ln1_b, ffn_w_gate, ffn_w_up, ffn_conv_w, ffn_conv_b, ffn_w_down, ln2_g, ln2_b):
    _, s, dm = x.shape
    dk = gla_gate_w.shape[2]
    dv = w_proj_b.shape[1]
    n_groups = len(DILATIONS)
    a_cols = 3 * A_WIDTH * n_groups
    gla_cols = 2 * dk + 2 * dv
    low0 = a_cols + gla_cols
    bias = _band_bias(rel_bias)
    w_t = jnp.swapaxes(w_in, 1, 2)

    xf = x.reshape(s, dm)
    xb = xf.astype(BF16)
    for i in range(DEPTH):
        gate_w = jnp.pad(gla_gate_w[i], ((0, LANES - G_RANK), (0, 0)))
        outs, lses = [], []
        for g, d in enumerate(DILATIONS):
            qkv = _proj(xb, w_t, i, 3 * A_WIDTH * g, 3 * A_WIDTH, d, BF16)
            o, lse = _attention(qkv, bias, g)
            outs.append(o)
            lses.append(lse)
        ya = _combine(outs, lses, w_proj_a[i].astype(BF16))
        gla_in = _proj(xb, w_t, i, a_cols, gla_cols, 1, F32)
        cum = _gate(xb, w_t, i, low0, gate_w, gla_gate_b[i].reshape(1, dk))
        a = _gla(gla_in, cum, gla_norm_g[i].reshape(1, -1))
        gates = _proj(xb, w_t, i, low0 + G_RANK, 2 * dm, 1, BF16, act="sigmoid")
        y = _merge(a, w_proj_b[i].astype(BF16), ya, gates)
        xf, xb = _out_ln(y, w_out[i].astype(BF16), xf, ln1_g[i].reshape(1, dm), ln1_b[i].reshape(1, dm))
        h = _ffn1(xb, ffn_w_gate, ffn_w_up, ffn_conv_w, ffn_conv_b.reshape(DEPTH, 1, -1), i)
        xf, xb = _ffn2(h, ffn_w_down[i].astype(BF16), xf, ln2_g[i].reshape(1, dm), ln2_b[i].reshape(1, dm))
    return xf.reshape(x.shape)
```

```python
import functools
import math

import jax
import jax.numpy as jnp
from jax import lax
from jax.experimental import pallas as pl
from jax.experimental.pallas import tpu as pltpu

F32 = jnp.float32
BF16 = jnp.bfloat16

DEPTH = 4
DILATIONS = (1, 4, 16)
A_HEADS = 8
A_HEAD_DIM = 128
A_WIDTH = A_HEADS * A_HEAD_DIM
A_SPAN = 128
A_BLOCK = 128
A_STEP_BLOCKS = 4
G_HEADS = 4
G_RANK = 16
G_TAU = 16.0
G_CHUNK = 64
N_BUCKETS = 32
MAX_DIST = 2048
CONV_W = 3
LN_EPS = 1e-5
ALPHA = (2 * DEPTH) ** 0.25
NEG = -1e30
LANES = 128
SUBLANES = 8
MM_ROWS = 1024
VMEM_LIMIT = 56 << 20


def _params(*sem):
    return pltpu.CompilerParams(dimension_semantics=sem, vmem_limit_bytes=VMEM_LIMIT)


def _resident(shape, index_map):
    return pl.BlockSpec(shape, index_map, pipeline_mode=pl.Buffered(1))


def _nt(a, b):
    return lax.dot_general(a, b, (((1,), (1,)), ((), ())), preferred_element_type=F32)


def _tn(a, b):
    return lax.dot_general(a, b, (((0,), (0,)), ((), ())), preferred_element_type=F32)


def _dot(a, b):
    return jnp.dot(a, b, preferred_element_type=F32)


def _sigmoid(x):
    return 1.0 / (1.0 + jnp.exp(-x))


def _split2(a):
    hi = a.astype(BF16)
    lo = (a - hi.astype(F32)).astype(BF16)
    return hi, lo


def _layernorm(z, g, b):
    mu = jnp.mean(z, axis=-1, keepdims=True)
    zc = z - mu
    var = jnp.mean(zc * zc, axis=-1, keepdims=True)
    return zc * lax.rsqrt(var + LN_EPS) * g + b


def _rows(start, size, stride):
    return pl.ds(start, size) if stride == 1 else pl.ds(start, size, stride=stride)


def _proj_kernel(*refs, d, act, shift, sub):
    refs = list(refs)
    x_ref, w_ref = refs.pop(0), refs.pop(0)
    wn_ref = refs.pop(0) if shift else None
    o_ref, wb_ref = refs.pop(0), refs.pop(0)

    @pl.when(pl.program_id(1) == 0)
    def _():
        w = w_ref[0]
        if shift:
            tn = w.shape[0]
            w = jnp.concatenate([w, wn_ref[0]], axis=0)[shift:shift + tn]
        wb_ref[...] = w.astype(BF16)

    acc_ref = refs.pop(0) if d > 1 else None
    for part in range(x_ref.shape[0] // sub):
        acc = _nt(x_ref[part * sub:(part + 1) * sub, :], wb_ref[...])
        if act == "sigmoid":
            acc = _sigmoid(acc)
        if d == 1:
            o_ref[0, part * sub:(part + 1) * sub, :] = acc.astype(o_ref.dtype)
            continue
        n = sub // d
        for c in range(acc.shape[1] // LANES):
            cs = slice(c * LANES, (c + 1) * LANES)
            acc_ref[c] = acc[:, cs]
            for r in range(d):
                o_ref[r, part * n:(part + 1) * n, cs] = (
                    acc_ref[c, pl.ds(r, n, stride=d), :].astype(o_ref.dtype))


def _proj(xb, w_t, layer, col0, ncols, d, out_dtype, *, act=None, tm=2048, tn=1024):
    s, dm = xb.shape
    tm, tn = min(tm, s), min(tn, ncols)
    sub = min(tm, MM_ROWS)
    shift = col0 % LANES
    c0 = (col0 - shift) // tn
    assert col0 - shift == c0 * tn and ncols % tn == 0 and s % tm == 0 and tm % sub == 0
    assert sub % (2 * SUBLANES * d) == 0 and shift % SUBLANES == 0
    in_specs = [pl.BlockSpec((tm, dm), lambda j, m: (m, 0)),
                pl.BlockSpec((1, tn, dm), lambda j, m: (layer, c0 + j, 0))]
    operands = [xb, w_t]
    if shift:
        per = tn // LANES
        in_specs.append(pl.BlockSpec((1, LANES, dm), lambda j, m: (layer, (c0 + j + 1) * per, 0)))
        operands.append(w_t)
    scratch = [pltpu.VMEM((tn, dm), BF16)] + ([pltpu.VMEM((tn // LANES, sub, LANES), F32)] if d > 1 else [])
    return pl.pallas_call(
        functools.partial(_proj_kernel, d=d, act=act, shift=shift, sub=sub),
        grid=(ncols // tn, s // tm),
        in_specs=in_specs,
        out_specs=pl.BlockSpec((d, tm // d, tn), lambda j, m: (0, m, j)),
        out_shape=jax.ShapeDtypeStruct((d, s // d, ncols), out_dtype),
        scratch_shapes=scratch,
        compiler_params=_params("arbitrary", "arbitrary"),
        name=f"proj_d{d}_{ncols}",
    )(*operands)


def _gate_kernel(x_ref, wl_ref, gw_ref, gb_ref, cum_ref, *, n_chunks):
    g = _nt(x_ref[...], wl_ref[0].astype(BF16))
    g_hi, g_lo = _split2(g)
    w_hi, w_lo = _split2(gw_ref[...])
    z = _dot(g_hi, w_hi) + _dot(g_hi, w_lo) + _dot(g_lo, w_hi) + gb_ref[...]
    log_a = (jnp.minimum(z, 0.0) - jnp.log(1.0 + jnp.exp(-jnp.abs(z)))) * (1.0 / G_TAU)
    row = lax.broadcasted_iota(jnp.int32, (G_CHUNK, G_CHUNK), 0)
    col = lax.broadcasted_iota(jnp.int32, (G_CHUNK, G_CHUNK), 1)
    tril = jnp.where(row >= col, 1.0, 0.0).astype(BF16)
    for c in range(n_chunks):
        hi, lo = _split2(log_a[c * G_CHUNK:(c + 1) * G_CHUNK])
        cum_ref[c * G_CHUNK:(c + 1) * G_CHUNK, :] = _dot(tril, hi) + _dot(tril, lo)


def _gate(xb, w_t, layer, low0, gate_w, gate_b, tm=512):
    s, dm = xb.shape
    dk = gate_w.shape[1]
    tm = min(tm, s)
    assert low0 % LANES == 0
    return pl.pallas_call(
        functools.partial(_gate_kernel, n_chunks=tm // G_CHUNK),
        grid=(s // tm,),
        in_specs=[pl.BlockSpec((tm, dm), lambda m: (m, 0)),
                  pl.BlockSpec((1, LANES, dm), lambda m: (layer, low0 // LANES, 0)),
                  pl.BlockSpec((LANES, dk), lambda m: (0, 0)),
                  pl.BlockSpec((1, dk), lambda m: (0, 0))],
        out_specs=pl.BlockSpec((tm, dk), lambda m: (m, 0)),
        out_shape=jax.ShapeDtypeStruct((s, dk), F32),
        compiler_params=_params("parallel"),
        name="gla_gate",
    )(xb, w_t, gate_w, gate_b)


def _gla_kernel(q_ref, k_ref, v_ref, c_ref, r_ref, ng_ref, o_ref, st_ref, *, n_chunks, scale):
    @pl.when(pl.program_id(1) == 0)
    def _():
        st_ref[...] = jnp.zeros_like(st_ref)

    row = lax.broadcasted_iota(jnp.int32, (G_CHUNK, G_CHUNK), 0)
    col = lax.broadcasted_iota(jnp.int32, (G_CHUNK, G_CHUNK), 1)
    causal = row >= col
    ng = ng_ref[...]
    for c in range(n_chunks):
        sl = slice(c * G_CHUNK, (c + 1) * G_CHUNK)
        cum = c_ref[sl, :]
        last = cum[G_CHUNK - 1:G_CHUNK, :]
        q = q_ref[0, sl, :].astype(F32) * scale
        k = k_ref[0, sl, :].astype(F32)
        v = v_ref[0, sl, :].astype(BF16)
        qd = (q * jnp.exp(cum)).astype(BF16)
        ki = (k * jnp.exp(-cum)).astype(BF16)
        ke = (k * jnp.exp(last - cum)).astype(BF16)
        att = jnp.where(causal, _nt(qd, ki), 0.0)
        st = st_ref[...]
        o = _dot(att.astype(BF16), v) + _nt(qd, st.astype(BF16))
        st_ref[...] = st * jnp.exp(last) + _tn(v, ke)
        ms = jnp.mean(o * o, axis=-1, keepdims=True)
        on = o * lax.rsqrt(ms + LN_EPS) * ng
        gr = r_ref[0, sl, :].astype(F32)
        o_ref[sl, :] = (on * (gr * _sigmoid(gr))).astype(o_ref.dtype)


def _gla(gla_in, cum, norm_g, tg=512):
    s, dk = cum.shape
    dv = (gla_in.shape[2] - 2 * dk) // 2
    hk, hv = dk // G_HEADS, dv // G_HEADS
    tg = min(tg, s)
    kq = dk // hk
    kv = 2 * dk // hv
    kr = (2 * dk + dv) // hv
    return pl.pallas_call(
        functools.partial(_gla_kernel, n_chunks=tg // G_CHUNK, scale=hk ** -0.5),
        grid=(G_HEADS, s // tg),
        in_specs=[pl.BlockSpec((1, tg, hk), lambda h, j: (0, j, h)),
                  pl.BlockSpec((1, tg, hk), lambda h, j: (0, j, kq + h)),
                  pl.BlockSpec((1, tg, hv), lambda h, j: (0, j, kv + h)),
                  pl.BlockSpec((tg, hk), lambda h, j: (j, h)),
                  pl.BlockSpec((1, tg, hv), lambda h, j: (0, j, kr + h)),
                  pl.BlockSpec((1, hv), lambda h, j: (0, 0))],
        out_specs=pl.BlockSpec((tg, hv), lambda h, j: (j, h)),
        out_shape=jax.ShapeDtypeStruct((s, dv), BF16),
        scratch_shapes=[pltpu.VMEM((hv, hk), F32)],
        compiler_params=_params("parallel", "arbitrary"),
        name="gla",
    )(gla_in, gla_in, gla_in, cum, gla_in, norm_g)


def _attn_kernel(q_ref, kc_ref, vc_ref, kp_ref, vp_ref, b_ref, o_ref, lse_ref, *, d, rpg, nblk, scale):
    first = pl.program_id(0) == 0
    r0 = pl.program_id(1) * rpg if rpg < d else 0
    lane = lax.broadcasted_iota(jnp.int32, (A_BLOCK, LANES), 1)
    col = lax.broadcasted_iota(jnp.int32, (A_BLOCK, 2 * A_BLOCK), 1)
    for rl in range(rpg):
        for b in range(nblk):
            rows = slice(b * A_BLOCK, (b + 1) * A_BLOCK)
            prev = slice((b - 1) * A_BLOCK, b * A_BLOCK)
            pos = _rows(b * A_BLOCK * d + r0 + rl, A_BLOCK, d)
            lse_tile = jnp.zeros((A_BLOCK, LANES), F32)
            for h in range(A_HEADS):
                cs = slice(h * A_HEAD_DIM, (h + 1) * A_HEAD_DIM)
                q = q_ref[rl, rows, cs]
                if b == 0:
                    kp, vp = kp_ref[rl, :, cs], vp_ref[rl, :, cs]
                else:
                    kp, vp = kc_ref[rl, prev, cs], vc_ref[rl, prev, cs]
                kk = jnp.concatenate([kp, kc_ref[rl, rows, cs]], axis=0)
                vv = jnp.concatenate([vp, vc_ref[rl, rows, cs]], axis=0)
                sc = _nt(q, kk) * scale + b_ref[h]
                if b == 0:
                    sc = jnp.where(jnp.logical_and(first, col < A_BLOCK), NEG, sc)
                m = jnp.max(sc, axis=-1, keepdims=True)
                p = jnp.exp(sc - m)
                l = jnp.sum(p, axis=-1, keepdims=True)
                o_ref[h, pos, :] = _dot(p.astype(BF16), vv) / l
                lse_tile = jnp.where(lane == h, m + jnp.log(l), lse_tile)
            lse_ref[pos, :] = lse_tile


def _attention(qkv, bias, group):
    d, n_sub, _ = qkv.shape
    s = d * n_sub
    rpg = min(d, A_STEP_BLOCKS)
    nblk = max(1, min(A_STEP_BLOCKS // rpg, n_sub // A_BLOCK))
    tq = nblk * A_BLOCK
    cur = lambda c: pl.BlockSpec((rpg, tq, A_WIDTH), lambda j, g: (g, j, c))
    prv = lambda c: pl.BlockSpec((rpg, A_BLOCK, A_WIDTH),
                                 lambda j, g: (g, jnp.maximum(j * nblk - 1, 0), c))
    return pl.pallas_call(
        functools.partial(_attn_kernel, d=d, rpg=rpg, nblk=nblk, scale=A_HEAD_DIM ** -0.5),
        grid=(n_sub // tq, d // rpg),
        in_specs=[cur(0), cur(1), cur(2), prv(1), prv(2),
                  _resident((A_HEADS, A_BLOCK, 2 * A_BLOCK), lambda j, g: (group, 0, 0))],
        out_specs=[pl.BlockSpec((A_HEADS, d * tq, A_HEAD_DIM), lambda j, g: (0, j, 0)),
                   pl.BlockSpec((d * tq, LANES), lambda j, g: (j, 0))],
        out_shape=[jax.ShapeDtypeStruct((A_HEADS, s, A_HEAD_DIM), F32),
                   jax.ShapeDtypeStruct((s, LANES), F32)],
        compiler_params=_params("parallel", "arbitrary"),
        name=f"attn_d{d}",
    )(qkv, qkv, qkv, qkv, qkv, bias)


def _combine_kernel(o0_ref, o1_ref, o2_ref, l0_ref, l1_ref, l2_ref, w_ref, y_ref):
    l0, l1, l2 = l0_ref[...], l1_ref[...], l2_ref[...]
    m = jnp.maximum(jnp.maximum(l0, l1), l2)
    e0, e1, e2 = jnp.exp(l0 - m), jnp.exp(l1 - m), jnp.exp(l2 - m)
    inv = 1.0 / (e0 + e1 + e2)
    w0, w1, w2 = e0 * inv, e1 * inv, e2 * inv
    pieces = []
    for h in range(A_HEADS):
        ya = w0[:, h:h + 1] * o0_ref[h] + w1[:, h:h + 1] * o1_ref[h] + w2[:, h:h + 1] * o2_ref[h]
        pieces.append(ya.astype(BF16))
    y_ref[...] = _dot(jnp.concatenate(pieces, axis=1), w_ref[0]).astype(y_ref.dtype)


def _combine(outs, lses, w_proj_a, layer, tm=512):
    s = outs[0].shape[1]
    dm = w_proj_a.shape[2]
    tm = min(tm, s)
    o_spec = pl.BlockSpec((A_HEADS, tm, A_HEAD_DIM), lambda m: (0, m, 0))
    l_spec = pl.BlockSpec((tm, LANES), lambda m: (m, 0))
    return pl.pallas_call(
        _combine_kernel,
        grid=(s // tm,),
        in_specs=[o_spec, o_spec, o_spec, l_spec, l_spec, l_spec,
                  _resident((1, A_WIDTH, dm), lambda m: (layer, 0, 0))],
        out_specs=pl.BlockSpec((tm, dm), lambda m: (m, 0)),
        out_shape=jax.ShapeDtypeStruct((s, dm), F32),
        compiler_params=_params("parallel"),
        name="combine_proj_a",
    )(*outs, *lses, w_proj_a)


def _merge_kernel(a_ref, w_ref, ya_ref, sa_ref, sb_ref, y_ref, wb_ref):
    @pl.when(pl.program_id(1) == 0)
    def _():
        wb_ref[...] = w_ref[0].astype(BF16)

    yb = _dot(a_ref[...], wb_ref[...])
    y = sa_ref[0].astype(F32) * ya_ref[...] + sb_ref[0].astype(F32) * yb
    y_ref[...] = y.astype(y_ref.dtype)


def _merge(a, w_proj_b, layer, ya, gates, tm=1024, tn=1024):
    s, dv = a.shape
    dm = w_proj_b.shape[2]
    tm, tn = min(tm, s), min(tn, dm)
    nb = dm // tn
    return pl.pallas_call(
        _merge_kernel,
        grid=(nb, s // tm),
        in_specs=[pl.BlockSpec((tm, dv), lambda j, m: (m, 0)),
                  pl.BlockSpec((1, dv, tn), lambda j, m: (layer, 0, j)),
                  pl.BlockSpec((tm, tn), lambda j, m: (m, j)),
                  pl.BlockSpec((1, tm, tn), lambda j, m: (0, m, j)),
                  pl.BlockSpec((1, tm, tn), lambda j, m: (0, m, nb + j))],
        out_specs=pl.BlockSpec((tm, tn), lambda j, m: (m, j)),
        out_shape=jax.ShapeDtypeStruct((s, dm), BF16),
        scratch_shapes=[pltpu.VMEM((dv, tn), BF16)],
        compiler_params=_params("arbitrary", "arbitrary"),
        name="proj_b_merge",
    )(a, w_proj_b, ya, gates, gates)


def _out_ln_kernel(y_ref, w_ref, x_ref, g_ref, b_ref, xo_ref, xb_ref):
    z = ALPHA * x_ref[...] + _dot(y_ref[...], w_ref[0])
    xn = _layernorm(z, g_ref[...], b_ref[...])
    xo_ref[...] = xn
    xb_ref[...] = xn.astype(BF16)


def _out_ln(y, w_out, layer, x, g, b, tm=512):
    s, dm = x.shape
    tm = min(tm, s)
    row = pl.BlockSpec((tm, dm), lambda m: (m, 0))
    vec = pl.BlockSpec((1, dm), lambda m: (0, 0))
    return pl.pallas_call(
        _out_ln_kernel,
        grid=(s // tm,),
        in_specs=[row, _resident((1, dm, dm), lambda m: (layer, 0, 0)), row, vec, vec],
        out_specs=[row, row],
        out_shape=[jax.ShapeDtypeStruct((s, dm), F32), jax.ShapeDtypeStruct((s, dm), BF16)],
        compiler_params=_params("parallel"),
        name="out_proj_ln",
    )(y, w_out, x, g, b)


def _ffn1_kernel(x_ref, wg_ref, wu_ref, cw_ref, cb_ref, h_ref, wgb_ref, wub_ref, halo_ref):
    tm = x_ref.shape[0]

    @pl.when(pl.program_id(1) == 0)
    def _():
        wgb_ref[...] = wg_ref[0].astype(BF16)
        wub_ref[...] = wu_ref[0].astype(BF16)
        halo_ref[...] = jnp.zeros_like(halo_ref)

    cw = cw_ref[0]
    sub = min(tm, MM_ROWS)
    for part in range(tm // sub):
        rows = slice(part * sub, (part + 1) * sub)
        x = x_ref[rows, :]
        g = _dot(x, wgb_ref[...])
        u = _dot(x, wub_ref[...])
        halo = halo_ref[...]
        row = lax.broadcasted_iota(jnp.int32, g.shape, 0)
        g1 = jnp.where(row == 0, halo[7:8], pltpu.roll(g, 1, axis=0))
        g2 = jnp.where(row == 0, halo[6:7], jnp.where(row == 1, halo[7:8], pltpu.roll(g, 2, axis=0)))
        halo_ref[...] = g[sub - 8:sub]
        gc = cb_ref[0] + cw[0:1] * g2 + cw[1:2] * g1 + cw[2:3] * g
        h_ref[rows, :] = (gc * _sigmoid(gc) * u).astype(h_ref.dtype)


def _ffn1(xb, w_gate, w_up, conv_w, conv_b, layer, tm=2048, tn=512):
    s, dm = xb.shape
    dff = w_gate.shape[2]
    tm = min(tm, s)
    wspec = pl.BlockSpec((1, dm, tn), lambda j, m: (layer, 0, j))
    return pl.pallas_call(
        _ffn1_kernel,
        grid=(dff // tn, s // tm),
        in_specs=[pl.BlockSpec((tm, dm), lambda j, m: (m, 0)), wspec, wspec,
                  pl.BlockSpec((1, CONV_W, tn), lambda j, m: (layer, 0, j)),
                  pl.BlockSpec((1, 1, tn), lambda j, m: (layer, 0, j))],
        out_specs=pl.BlockSpec((tm, tn), lambda j, m: (m, j)),
        out_shape=jax.ShapeDtypeStruct((s, dff), BF16),
        scratch_shapes=[pltpu.VMEM((dm, tn), BF16), pltpu.VMEM((dm, tn), BF16),
                        pltpu.VMEM((8, tn), F32)],
        compiler_params=_params("arbitrary", "arbitrary"),
        name="ffn_gate_up_conv",
    )(xb, w_gate, w_up, conv_w, conv_b)


def _ffn2_kernel(h_ref, w_ref, x_ref, g_ref, b_ref, xo_ref, xb_ref):
    k = pl.program_id(1)

    @pl.when(k == 0)
    def _():
        xo_ref[...] = ALPHA * x_ref[...]

    slab = 512
    for c in range(xo_ref.shape[1] // slab):
        cs = slice(c * slab, (c + 1) * slab)
        xo_ref[:, cs] += _dot(h_ref[...], w_ref[0, :, cs])

    @pl.when(k == pl.num_programs(1) - 1)
    def _():
        xn = _layernorm(xo_ref[...], g_ref[...], b_ref[...])
        xo_ref[...] = xn
        xb_ref[...] = xn.astype(BF16)


def _ffn2(h, w_down, layer, x, g, b, tm=1024, tk=512):
    s, dm = x.shape
    dff = h.shape[1]
    tm = min(tm, s)
    row = pl.BlockSpec((tm, dm), lambda m, k: (m, 0))
    vec = pl.BlockSpec((1, dm), lambda m, k: (0, 0))
    return pl.pallas_call(
        _ffn2_kernel,
        grid=(s // tm, dff // tk),
        in_specs=[pl.BlockSpec((tm, tk), lambda m, k: (m, k)),
                  pl.BlockSpec((1, tk, dm), lambda m, k: (layer, k, 0)), row, vec, vec],
        out_specs=[row, row],
        out_shape=[jax.ShapeDtypeStruct((s, dm), F32), jax.ShapeDtypeStruct((s, dm), BF16)],
        compiler_params=_params("parallel", "arbitrary"),
        name="ffn_down_ln",
    )(h, w_down, x, g, b)


def _band_buckets(dilation):
    qi = jnp.arange(A_BLOCK)[:, None]
    ci = jnp.arange(2 * A_BLOCK)[None, :]
    off = A_BLOCK + qi - ci
    valid = (off >= 0) & (off <= A_SPAN)
    dist = dilation * jnp.clip(off, 0, A_SPAN)
    max_exact = N_BUCKETS // 2
    df = jnp.maximum(dist, 1).astype(F32)
    large = max_exact + (jnp.log(df / max_exact) / math.log(MAX_DIST / max_exact)
                         * (N_BUCKETS - max_exact)).astype(jnp.int32)
    bucket = jnp.where(dist < max_exact, dist, jnp.minimum(large, N_BUCKETS - 1))
    return jnp.where(valid, bucket, -1).astype(jnp.int32)


def _bias_kernel(tab_ref, idx_ref, o_ref):
    g = pl.program_id(0)
    idx = idx_ref[0]
    for h in range(A_HEADS):
        acc = jnp.full(idx.shape, NEG, F32)
        for b in range(N_BUCKETS):
            acc = jnp.where(idx == b, tab_ref[b, g * A_HEADS + h], acc)
        o_ref[h] = acc


def _band_bias(rel_bias):
    idx = jnp.stack([_band_buckets(d) for d in DILATIONS])
    n_groups = len(DILATIONS)
    return pl.pallas_call(
        _bias_kernel,
        grid=(n_groups,),
        in_specs=[pl.BlockSpec(memory_space=pltpu.SMEM),
                  pl.BlockSpec((1, A_BLOCK, 2 * A_BLOCK), lambda g: (g, 0, 0))],
        out_specs=pl.BlockSpec((A_HEADS, A_BLOCK, 2 * A_BLOCK), lambda g: (g, 0, 0)),
        out_shape=jax.ShapeDtypeStruct((n_groups * A_HEADS, A_BLOCK, 2 * A_BLOCK), F32),
        compiler_params=_params("parallel"),
        name="band_bias",
    )(rel_bias, idx)


def kernel(x, w_in, gla_gate_w, gla_gate_b, gla_norm_g, w_proj_a, w_proj_b, w_out, rel_bias,
           ln1_g, ln1_b, ffn_w_gate, ffn_w_up, ffn_conv_w, ffn_conv_b, ffn_w_down, ln2_g, ln2_b):
    _, s, dm = x.shape
    dk = gla_gate_w.shape[2]
    dv = w_proj_b.shape[1]
    n_groups = len(DILATIONS)
    a_cols = 3 * A_WIDTH * n_groups
    gla_cols = 2 * dk + 2 * dv
    low0 = a_cols + gla_cols
    bias = _band_bias(rel_bias)
    w_t = jnp.swapaxes(w_in, 1, 2)

    w_a, w_o, w_d = w_proj_a.astype(BF16), w_out.astype(BF16), ffn_w_down.astype(BF16)

    xf = x.reshape(s, dm)
    xb = xf.astype(BF16)
    for i in range(DEPTH):
        gate_w = jnp.pad(gla_gate_w[i], ((0, LANES - G_RANK), (0, 0)))
        outs, lses = [], []
        for g, d in enumerate(DILATIONS):
            qkv = _proj(xb, w_t, i, 3 * A_WIDTH * g, 3 * A_WIDTH, d, BF16)
            o, lse = _attention(qkv, bias, g)
            outs.append(o)
            lses.append(lse)
        ya = _combine(outs, lses, w_a, i)
        gla_in = _proj(xb, w_t, i, a_cols, gla_cols, 1, F32)
        cum = _gate(xb, w_t, i, low0, gate_w, gla_gate_b[i].reshape(1, dk))
        a = _gla(gla_in, cum, gla_norm_g[i].reshape(1, -1))
        gates = _proj(xb, w_t, i, low0 + G_RANK, 2 * dm, 1, BF16, act="sigmoid")
        y = _merge(a, w_proj_b, i, ya, gates)
        xf, xb = _out_ln(y, w_o, i, xf, ln1_g[i].reshape(1, dm), ln1_b[i].reshape(1, dm))
        h = _ffn1(xb, ffn_w_gate, ffn_w_up, ffn_conv_w, ffn_conv_b.reshape(DEPTH, 1, -1), i)
        xf, xb = _ffn2(h, w_d, i, xf, ln2_g[i].reshape(1, dm), ln2_b[i].reshape(1, dm))
    return xf.reshape(x.shape)
```

```python
import functools
import math

import jax
import jax.numpy as jnp
from jax import lax
from jax.experimental import pallas as pl
from jax.experimental.pallas import tpu as pltpu

F32 = jnp.float32
BF16 = jnp.bfloat16

DEPTH = 4
DILATIONS = (1, 4, 16)
A_HEADS = 8
A_HEAD_DIM = 128
A_WIDTH = A_HEADS * A_HEAD_DIM
A_SPAN = 128
A_BLOCK = 128
A_STEP_BLOCKS = 4
G_HEADS = 4
G_RANK = 16
G_TAU = 16.0
G_CHUNK = 64
N_BUCKETS = 32
MAX_DIST = 2048
CONV_W = 3
LN_EPS = 1e-5
ALPHA = (2 * DEPTH) ** 0.25
NEG = -1e30
LANES = 128
SUBLANES = 8
MM_ROWS = 1024
VMEM_LIMIT = 56 << 20


def _params(*sem):
    return pltpu.CompilerParams(dimension_semantics=sem, vmem_limit_bytes=VMEM_LIMIT)


def _resident(shape, index_map):
    return pl.BlockSpec(shape, index_map, pipeline_mode=pl.Buffered(1))


def _nt(a, b):
    return lax.dot_general(a, b, (((1,), (1,)), ((), ())), preferred_element_type=F32)


def _tn(a, b):
    return lax.dot_general(a, b, (((0,), (0,)), ((), ())), preferred_element_type=F32)


def _dot(a, b):
    return jnp.dot(a, b, preferred_element_type=F32)


def _sigmoid(x):
    return 1.0 / (1.0 + jnp.exp(-x))


def _split2(a):
    hi = a.astype(BF16)
    lo = (a - hi.astype(F32)).astype(BF16)
    return hi, lo


def _layernorm(z, g, b):
    mu = jnp.mean(z, axis=-1, keepdims=True)
    zc = z - mu
    var = jnp.mean(zc * zc, axis=-1, keepdims=True)
    return zc * lax.rsqrt(var + LN_EPS) * g + b


def _rows(start, size, stride):
    return pl.ds(start, size) if stride == 1 else pl.ds(start, size, stride=stride)


def _proj_kernel(*refs, d, act, shift, sub):
    refs = list(refs)
    x_ref, w_ref = refs.pop(0), refs.pop(0)
    wn_ref = refs.pop(0) if shift else None
    o_ref, wb_ref = refs.pop(0), refs.pop(0)

    @pl.when(pl.program_id(1) == 0)
    def _():
        w = w_ref[0]
        if shift:
            tn = w.shape[0]
            w = jnp.concatenate([w, wn_ref[0]], axis=0)[shift:shift + tn]
        wb_ref[...] = w.astype(BF16)

    acc_ref = refs.pop(0) if d > 1 else None
    for part in range(x_ref.shape[0] // sub):
        acc = _nt(x_ref[part * sub:(part + 1) * sub, :], wb_ref[...])
        if act == "sigmoid":
            acc = _sigmoid(acc)
        if d == 1:
            o_ref[0, part * sub:(part + 1) * sub, :] = acc.astype(o_ref.dtype)
            continue
        n = sub // d
        for c in range(acc.shape[1] // LANES):
            cs = slice(c * LANES, (c + 1) * LANES)
            acc_ref[c] = acc[:, cs]
            for r in range(d):
                o_ref[r, part * n:(part + 1) * n, cs] = (
                    acc_ref[c, pl.ds(r, n, stride=d), :].astype(o_ref.dtype))


def _proj(xb, w_t, layer, col0, ncols, d, out_dtype, *, act=None, tm=2048, tn=1024):
    s, dm = xb.shape
    tm, tn = min(tm, s), min(tn, ncols)
    sub = min(tm, MM_ROWS)
    shift = col0 % LANES
    c0 = (col0 - shift) // tn
    assert col0 - shift == c0 * tn and ncols % tn == 0 and s % tm == 0 and tm % sub == 0
    assert sub % (2 * SUBLANES * d) == 0 and shift % SUBLANES == 0
    in_specs = [pl.BlockSpec((tm, dm), lambda j, m: (m, 0)),
                pl.BlockSpec((1, tn, dm), lambda j, m: (layer, c0 + j, 0))]
    operands = [xb, w_t]
    if shift:
        per = tn // LANES
        in_specs.append(pl.BlockSpec((1, LANES, dm), lambda j, m: (layer, (c0 + j + 1) * per, 0)))
        operands.append(w_t)
    scratch = [pltpu.VMEM((tn, dm), BF16)] + ([pltpu.VMEM((tn // LANES, sub, LANES), F32)] if d > 1 else [])
    return pl.pallas_call(
        functools.partial(_proj_kernel, d=d, act=act, shift=shift, sub=sub),
        grid=(ncols // tn, s // tm),
        in_specs=in_specs,
        out_specs=pl.BlockSpec((d, tm // d, tn), lambda j, m: (0, m, j)),
        out_shape=jax.ShapeDtypeStruct((d, s // d, ncols), out_dtype),
        scratch_shapes=scratch,
        compiler_params=_params("arbitrary", "arbitrary"),
        name=f"proj_d{d}_{ncols}",
    )(*operands)


def _gate_kernel(x_ref, wl_ref, gw_ref, gb_ref, cum_ref, *, n_chunks):
    g = _nt(x_ref[...], wl_ref[0].astype(BF16))
    g_hi, g_lo = _split2(g)
    w_hi, w_lo = _split2(gw_ref[...])
    z = _dot(g_hi, w_hi) + _dot(g_hi, w_lo) + _dot(g_lo, w_hi) + gb_ref[...]
    log_a = (jnp.minimum(z, 0.0) - jnp.log(1.0 + jnp.exp(-jnp.abs(z)))) * (1.0 / G_TAU)
    row = lax.broadcasted_iota(jnp.int32, (G_CHUNK, G_CHUNK), 0)
    col = lax.broadcasted_iota(jnp.int32, (G_CHUNK, G_CHUNK), 1)
    tril = jnp.where(row >= col, 1.0, 0.0).astype(BF16)
    for c in range(n_chunks):
        hi, lo = _split2(log_a[c * G_CHUNK:(c + 1) * G_CHUNK])
        cum_ref[c * G_CHUNK:(c + 1) * G_CHUNK, :] = _dot(tril, hi) + _dot(tril, lo)


def _gate(xb, w_t, layer, low0, gate_w, gate_b, tm=512):
    s, dm = xb.shape
    dk = gate_w.shape[1]
    tm = min(tm, s)
    assert low0 % LANES == 0
    return pl.pallas_call(
        functools.partial(_gate_kernel, n_chunks=tm // G_CHUNK),
        grid=(s // tm,),
        in_specs=[pl.BlockSpec((tm, dm), lambda m: (m, 0)),
                  pl.BlockSpec((1, LANES, dm), lambda m: (layer, low0 // LANES, 0)),
                  pl.BlockSpec((LANES, dk), lambda m: (0, 0)),
                  pl.BlockSpec((1, dk), lambda m: (0, 0))],
        out_specs=pl.BlockSpec((tm, dk), lambda m: (m, 0)),
        out_shape=jax.ShapeDtypeStruct((s, dk), F32),
        compiler_params=_params("parallel"),
        name="gla_gate",
    )(xb, w_t, gate_w, gate_b)


def _gla_kernel(q_ref, k_ref, v_ref, c_ref, r_ref, ng_ref, o_ref, st_ref, *, n_chunks, scale):
    @pl.when(pl.program_id(1) == 0)
    def _():
        st_ref[...] = jnp.zeros_like(st_ref)

    row = lax.broadcasted_iota(jnp.int32, (G_CHUNK, G_CHUNK), 0)
    col = lax.broadcasted_iota(jnp.int32, (G_CHUNK, G_CHUNK), 1)
    causal = row >= col
    ng = ng_ref[...]
    for c in range(n_chunks):
        sl = slice(c * G_CHUNK, (c + 1) * G_CHUNK)
        cum = c_ref[sl, :]
        last = cum[G_CHUNK - 1:G_CHUNK, :]
        q = q_ref[0, sl, :].astype(F32) * scale
        k = k_ref[0, sl, :].astype(F32)
        v = v_ref[0, sl, :].astype(BF16)
        qd = (q * jnp.exp(cum)).astype(BF16)
        ki = (k * jnp.exp(-cum)).astype(BF16)
        ke = (k * jnp.exp(last - cum)).astype(BF16)
        att = jnp.where(causal, _nt(qd, ki), 0.0)
        st = st_ref[...]
        o = _dot(att.astype(BF16), v) + _nt(qd, st.astype(BF16))
        st_ref[...] = st * jnp.exp(last) + _tn(v, ke)
        ms = jnp.mean(o * o, axis=-1, keepdims=True)
        on = o * lax.rsqrt(ms + LN_EPS) * ng
        gr = r_ref[0, sl, :].astype(F32)
        o_ref[sl, :] = (on * (gr * _sigmoid(gr))).astype(o_ref.dtype)


def _gla(gla_in, cum, norm_g, tg=512):
    s, dk = cum.shape
    dv = (gla_in.shape[2] - 2 * dk) // 2
    hk, hv = dk // G_HEADS, dv // G_HEADS
    tg = min(tg, s)
    kq = dk // hk
    kv = 2 * dk // hv
    kr = (2 * dk + dv) // hv
    return pl.pallas_call(
        functools.partial(_gla_kernel, n_chunks=tg // G_CHUNK, scale=hk ** -0.5),
        grid=(G_HEADS, s // tg),
        in_specs=[pl.BlockSpec((1, tg, hk), lambda h, j: (0, j, h)),
                  pl.BlockSpec((1, tg, hk), lambda h, j: (0, j, kq + h)),
                  pl.BlockSpec((1, tg, hv), lambda h, j: (0, j, kv + h)),
                  pl.BlockSpec((tg, hk), lambda h, j: (j, h)),
                  pl.BlockSpec((1, tg, hv), lambda h, j: (0, j, kr + h)),
                  pl.BlockSpec((1, hv), lambda h, j: (0, 0))],
        out_specs=pl.BlockSpec((tg, hv), lambda h, j: (j, h)),
        out_shape=jax.ShapeDtypeStruct((s, dv), BF16),
        scratch_shapes=[pltpu.VMEM((hv, hk), F32)],
        compiler_params=_params("parallel", "arbitrary"),
        name="gla",
    )(gla_in, gla_in, gla_in, cum, gla_in, norm_g)


def _attn_kernel(q_ref, kc_ref, vc_ref, kp_ref, vp_ref, b_ref, o_ref, lse_ref, *, d, rpg, nblk, scale):
    first = pl.program_id(0) == 0
    r0 = pl.program_id(1) * rpg if rpg < d else 0
    head_row = lax.broadcasted_iota(jnp.int32, (A_HEADS, A_BLOCK), 0)
    pad = jnp.zeros((LANES - A_HEADS, A_BLOCK), F32)
    for rl in range(rpg):
        for b in range(nblk):
            rows = slice(b * A_BLOCK, (b + 1) * A_BLOCK)
            prev = slice((b - 1) * A_BLOCK, b * A_BLOCK)
            pos = _rows(b * A_BLOCK * d + r0 + rl, A_BLOCK, d)
            lse_t = jnp.zeros((A_HEADS, A_BLOCK), F32)
            for h in range(A_HEADS):
                cs = slice(h * A_HEAD_DIM, (h + 1) * A_HEAD_DIM)
                q = q_ref[rl, rows, cs]
                if b == 0:
                    kp, vp = kp_ref[rl, :, cs], vp_ref[rl, :, cs]
                else:
                    kp, vp = kc_ref[rl, prev, cs], vc_ref[rl, prev, cs]
                kk = jnp.concatenate([kp, kc_ref[rl, rows, cs]], axis=0)
                vv = jnp.concatenate([vp, vc_ref[rl, rows, cs]], axis=0)
                sc = _nt(kk, q) * scale + b_ref[h]
                if b == 0:
                    sc = jnp.concatenate([jnp.where(first, NEG, sc[:A_BLOCK]), sc[A_BLOCK:]], axis=0)
                m = jnp.max(sc, axis=0, keepdims=True)
                p = jnp.exp(sc - m)
                l = jnp.sum(p, axis=0, keepdims=True)
                o_t = _tn(vv, p.astype(BF16)) * (1.0 / l)
                o_ref[h, pos, :] = o_t.T
                lse_t = jnp.where(head_row == h, m + jnp.log(l), lse_t)
            lse_ref[pos, :] = jnp.concatenate([lse_t, pad], axis=0).T


def _attention(qkv, bias, group):
    d, n_sub, _ = qkv.shape
    s = d * n_sub
    rpg = min(d, A_STEP_BLOCKS)
    nblk = max(1, min(A_STEP_BLOCKS // rpg, n_sub // A_BLOCK))
    tq = nblk * A_BLOCK
    cur = lambda c: pl.BlockSpec((rpg, tq, A_WIDTH), lambda j, g: (g, j, c))
    prv = lambda c: pl.BlockSpec((rpg, A_BLOCK, A_WIDTH),
                                 lambda j, g: (g, jnp.maximum(j * nblk - 1, 0), c))
    return pl.pallas_call(
        functools.partial(_attn_kernel, d=d, rpg=rpg, nblk=nblk, scale=A_HEAD_DIM ** -0.5),
        grid=(n_sub // tq, d // rpg),
        in_specs=[cur(0), cur(1), cur(2), prv(1), prv(2),
                  _resident((A_HEADS, 2 * A_BLOCK, A_BLOCK), lambda j, g: (group, 0, 0))],
        out_specs=[pl.BlockSpec((A_HEADS, d * tq, A_HEAD_DIM), lambda j, g: (0, j, 0)),
                   pl.BlockSpec((d * tq, LANES), lambda j, g: (j, 0))],
        out_shape=[jax.ShapeDtypeStruct((A_HEADS, s, A_HEAD_DIM), F32),
                   jax.ShapeDtypeStruct((s, LANES), F32)],
        compiler_params=_params("parallel", "arbitrary"),
        name=f"attn_d{d}",
    )(qkv, qkv, qkv, qkv, qkv, bias)


def _combine_kernel(o0_ref, o1_ref, o2_ref, l0_ref, l1_ref, l2_ref, w_ref, y_ref):
    l0, l1, l2 = l0_ref[...], l1_ref[...], l2_ref[...]
    m = jnp.maximum(jnp.maximum(l0, l1), l2)
    e0, e1, e2 = jnp.exp(l0 - m), jnp.exp(l1 - m), jnp.exp(l2 - m)
    inv = 1.0 / (e0 + e1 + e2)
    w0, w1, w2 = e0 * inv, e1 * inv, e2 * inv
    pieces = []
    for h in range(A_HEADS):
        ya = w0[:, h:h + 1] * o0_ref[h] + w1[:, h:h + 1] * o1_ref[h] + w2[:, h:h + 1] * o2_ref[h]
        pieces.append(ya.astype(BF16))
    y_ref[...] = _dot(jnp.concatenate(pieces, axis=1), w_ref[0]).astype(y_ref.dtype)


def _combine(outs, lses, w_proj_a, layer, tm=512):
    s = outs[0].shape[1]
    dm = w_proj_a.shape[2]
    tm = min(tm, s)
    o_spec = pl.BlockSpec((A_HEADS, tm, A_HEAD_DIM), lambda m: (0, m, 0))
    l_spec = pl.BlockSpec((tm, LANES), lambda m: (m, 0))
    return pl.pallas_call(
        _combine_kernel,
        grid=(s // tm,),
        in_specs=[o_spec, o_spec, o_spec, l_spec, l_spec, l_spec,
                  _resident((1, A_WIDTH, dm), lambda m: (layer, 0, 0))],
        out_specs=pl.BlockSpec((tm, dm), lambda m: (m, 0)),
        out_shape=jax.ShapeDtypeStruct((s, dm), BF16),
        compiler_params=_params("parallel"),
        name="combine_proj_a",
    )(*outs, *lses, w_proj_a)


def _merge_kernel(a_ref, w_ref, ya_ref, sa_ref, sb_ref, y_ref, wb_ref):
    @pl.when(pl.program_id(1) == 0)
    def _():
        wb_ref[...] = w_ref[0].astype(BF16)

    yb = _dot(a_ref[...], wb_ref[...])
    y = sa_ref[0].astype(F32) * ya_ref[...].astype(F32) + sb_ref[0].astype(F32) * yb
    y_ref[...] = y.astype(y_ref.dtype)


def _merge(a, w_proj_b, layer, ya, gates, tm=1024, tn=1024):
    s, dv = a.shape
    dm = w_proj_b.shape[2]
    tm, tn = min(tm, s), min(tn, dm)
    nb = dm // tn
    return pl.pallas_call(
        _merge_kernel,
        grid=(nb, s // tm),
        in_specs=[pl.BlockSpec((tm, dv), lambda j, m: (m, 0)),
                  pl.BlockSpec((1, dv, tn), lambda j, m: (layer, 0, j)),
                  pl.BlockSpec((tm, tn), lambda j, m: (m, j)),
                  pl.BlockSpec((1, tm, tn), lambda j, m: (0, m, j)),
                  pl.BlockSpec((1, tm, tn), lambda j, m: (0, m, nb + j))],
        out_specs=pl.BlockSpec((tm, tn), lambda j, m: (m, j)),
        out_shape=jax.ShapeDtypeStruct((s, dm), BF16),
        scratch_shapes=[pltpu.VMEM((dv, tn), BF16)],
        compiler_params=_params("arbitrary", "arbitrary"),
        name="proj_b_merge",
    )(a, w_proj_b, ya, gates, gates)


def _out_ln_kernel(y_ref, w_ref, x_ref, g_ref, b_ref, xo_ref, xb_ref):
    z = ALPHA * x_ref[...] + _dot(y_ref[...], w_ref[0])
    xn = _layernorm(z, g_ref[...], b_ref[...])
    xo_ref[...] = xn
    xb_ref[...] = xn.astype(BF16)


def _out_ln(y, w_out, layer, x, g, b, tm=512):
    s, dm = x.shape
    tm = min(tm, s)
    row = pl.BlockSpec((tm, dm), lambda m: (m, 0))
    vec = pl.BlockSpec((1, dm), lambda m: (0, 0))
    return pl.pallas_call(
        _out_ln_kernel,
        grid=(s // tm,),
        in_specs=[row, _resident((1, dm, dm), lambda m: (layer, 0, 0)), row, vec, vec],
        out_specs=[row, row],
        out_shape=[jax.ShapeDtypeStruct((s, dm), F32), jax.ShapeDtypeStruct((s, dm), BF16)],
        compiler_params=_params("parallel"),
        name="out_proj_ln",
    )(y, w_out, x, g, b)


def _ffn1_kernel(x_ref, wg_ref, wu_ref, cw_ref, cb_ref, h_ref, wgb_ref, wub_ref, halo_ref):
    tm = x_ref.shape[0]

    @pl.when(pl.program_id(1) == 0)
    def _():
        wgb_ref[...] = wg_ref[0].astype(BF16)
        wub_ref[...] = wu_ref[0].astype(BF16)
        halo_ref[...] = jnp.zeros_like(halo_ref)

    cw = cw_ref[0]
    sub = min(tm, MM_ROWS)
    for part in range(tm // sub):
        rows = slice(part * sub, (part + 1) * sub)
        x = x_ref[rows, :]
        g = _dot(x, wgb_ref[...])
        u = _dot(x, wub_ref[...])
        halo = halo_ref[...]
        row = lax.broadcasted_iota(jnp.int32, g.shape, 0)
        g1 = jnp.where(row == 0, halo[7:8], pltpu.roll(g, 1, axis=0))
        g2 = jnp.where(row == 0, halo[6:7], jnp.where(row == 1, halo[7:8], pltpu.roll(g, 2, axis=0)))
        halo_ref[...] = g[sub - 8:sub]
        gc = cb_ref[0] + cw[0:1] * g2 + cw[1:2] * g1 + cw[2:3] * g
        h_ref[rows, :] = (gc * _sigmoid(gc) * u).astype(h_ref.dtype)


def _ffn1(xb, w_gate, w_up, conv_w, conv_b, layer, tm=2048, tn=512):
    s, dm = xb.shape
    dff = w_gate.shape[2]
    tm = min(tm, s)
    wspec = pl.BlockSpec((1, dm, tn), lambda j, m: (layer, 0, j))
    return pl.pallas_call(
        _ffn1_kernel,
        grid=(dff // tn, s // tm),
        in_specs=[pl.BlockSpec((tm, dm), lambda j, m: (m, 0)), wspec, wspec,
                  pl.BlockSpec((1, CONV_W, tn), lambda j, m: (layer, 0, j)),
                  pl.BlockSpec((1, 1, tn), lambda j, m: (layer, 0, j))],
        out_specs=pl.BlockSpec((tm, tn), lambda j, m: (m, j)),
        out_shape=jax.ShapeDtypeStruct((s, dff), BF16),
        scratch_shapes=[pltpu.VMEM((dm, tn), BF16), pltpu.VMEM((dm, tn), BF16),
                        pltpu.VMEM((8, tn), F32)],
        compiler_params=_params("arbitrary", "arbitrary"),
        name="ffn_gate_up_conv",
    )(xb, w_gate, w_up, conv_w, conv_b)


def _ffn2_kernel(h_ref, w_ref, x_ref, g_ref, b_ref, xo_ref, xb_ref):
    k = pl.program_id(1)

    @pl.when(k == 0)
    def _():
        xo_ref[...] = ALPHA * x_ref[...]

    slab = 512
    for c in range(xo_ref.shape[1] // slab):
        cs = slice(c * slab, (c + 1) * slab)
        xo_ref[:, cs] += _dot(h_ref[...], w_ref[0, :, cs])

    @pl.when(k == pl.num_programs(1) - 1)
    def _():
        xn = _layernorm(xo_ref[...], g_ref[...], b_ref[...])
        xo_ref[...] = xn
        xb_ref[...] = xn.astype(BF16)


def _ffn2(h, w_down, layer, x, g, b, tm=1024, tk=512):
    s, dm = x.shape
    dff = h.shape[1]
    tm = min(tm, s)
    row = pl.BlockSpec((tm, dm), lambda m, k: (m, 0))
    vec = pl.BlockSpec((1, dm), lambda m, k: (0, 0))
    return pl.pallas_call(
        _ffn2_kernel,
        grid=(s // tm, dff // tk),
        in_specs=[pl.BlockSpec((tm, tk), lambda m, k: (m, k)),
                  pl.BlockSpec((1, tk, dm), lambda m, k: (layer, k, 0)), row, vec, vec],
        out_specs=[row, row],
        out_shape=[jax.ShapeDtypeStruct((s, dm), F32), jax.ShapeDtypeStruct((s, dm), BF16)],
        compiler_params=_params("parallel", "arbitrary"),
        name="ffn_down_ln",
    )(h, w_down, x, g, b)


def _band_buckets(dilation):
    qi = jnp.arange(A_BLOCK)[None, :]
    ci = jnp.arange(2 * A_BLOCK)[:, None]
    off = A_BLOCK + qi - ci
    valid = (off >= 0) & (off <= A_SPAN)
    dist = dilation * jnp.clip(off, 0, A_SPAN)
    max_exact = N_BUCKETS // 2
    df = jnp.maximum(dist, 1).astype(F32)
    large = max_exact + (jnp.log(df / max_exact) / math.log(MAX_DIST / max_exact)
                         * (N_BUCKETS - max_exact)).astype(jnp.int32)
    bucket = jnp.where(dist < max_exact, dist, jnp.minimum(large, N_BUCKETS - 1))
    return jnp.where(valid, bucket, -1).astype(jnp.int32)


def _bias_kernel(tab_ref, idx_ref, o_ref):
    g = pl.program_id(0)
    idx = idx_ref[0]
    for h in range(A_HEADS):
        acc = jnp.full(idx.shape, NEG, F32)
        for b in range(N_BUCKETS):
            acc = jnp.where(idx == b, tab_ref[b, g * A_HEADS + h], acc)
        o_ref[h] = acc


def _band_bias(rel_bias):
    idx = jnp.stack([_band_buckets(d) for d in DILATIONS])
    n_groups = len(DILATIONS)
    return pl.pallas_call(
        _bias_kernel,
        grid=(n_groups,),
        in_specs=[pl.BlockSpec(memory_space=pltpu.SMEM),
                  pl.BlockSpec((1, 2 * A_BLOCK, A_BLOCK), lambda g: (g, 0, 0))],
        out_specs=pl.BlockSpec((A_HEADS, 2 * A_BLOCK, A_BLOCK), lambda g: (g, 0, 0)),
        out_shape=jax.ShapeDtypeStruct((n_groups * A_HEADS, 2 * A_BLOCK, A_BLOCK), F32),
        compiler_params=_params("parallel"),
        name="band_bias",
    )(rel_bias, idx)


def kernel(x, w_in, gla_gate_w, gla_gate_b, gla_norm_g, w_proj_a, w_proj_b, w_out, rel_bias,
           ln1_g, ln1_b, ffn_w_gate, ffn_w_up, ffn_conv_w, ffn_conv_b, ffn_w_down, ln2_g, ln2_b):
    _, s, dm = x.shape
    dk = gla_gate_w.shape[2]
    dv = w_proj_b.shape[1]
    n_groups = len(DILATIONS)
    a_cols = 3 * A_WIDTH * n_groups
    gla_cols = 2 * dk + 2 * dv
    low0 = a_cols + gla_cols
    bias = _band_bias(rel_bias)
    w_t = jnp.swapaxes(w_in, 1, 2)

    w_a, w_o, w_d = w_proj_a.astype(BF16), w_out.astype(BF16), ffn_w_down.astype(BF16)

    xf = x.reshape(s, dm)
    xb = xf.astype(BF16)
    for i in range(DEPTH):
        gate_w = jnp.pad(gla_gate_w[i], ((0, LANES - G_RANK), (0, 0)))
        outs, lses = [], []
        for g, d in enumerate(DILATIONS):
            qkv = _proj(xb, w_t, i, 3 * A_WIDTH * g, 3 * A_WIDTH, d, BF16)
            o, lse = _attention(qkv, bias, g)
            outs.append(o)
            lses.append(lse)
        ya = _combine(outs, lses, w_a, i)
        gla_in = _proj(xb, w_t, i, a_cols, gla_cols, 1, F32)
        cum = _gate(xb, w_t, i, low0, gate_w, gla_gate_b[i].reshape(1, dk))
        a = _gla(gla_in, cum, gla_norm_g[i].reshape(1, -1))
        gates = _proj(xb, w_t, i, low0 + G_RANK, 2 * dm, 1, BF16, act="sigmoid")
        y = _merge(a, w_proj_b, i, ya, gates)
        xf, xb = _out_ln(y, w_o, i, xf, ln1_g[i].reshape(1, dm), ln1_b[i].reshape(1, dm))
        h = _ffn1(xb, ffn_w_gate, ffn_w_up, ffn_conv_w, ffn_conv_b.reshape(DEPTH, 1, -1), i)
        xf, xb = _ffn2(h, w_d, i, xf, ln2_g[i].reshape(1, dm), ln2_b[i].reshape(1, dm))
    return xf.reshape(x.shape)
```

```python
import functools
import math

import jax
import jax.numpy as jnp
from jax import lax
from jax.experimental import pallas as pl
from jax.experimental.pallas import tpu as pltpu

F32 = jnp.float32
BF16 = jnp.bfloat16

DEPTH = 4
DILATIONS = (1, 4, 16)
A_HEADS = 8
A_HEAD_DIM = 128
A_WIDTH = A_HEADS * A_HEAD_DIM
A_SPAN = 128
A_BLOCK = 128
A_STEP_BLOCKS = 4
G_HEADS = 4
G_RANK = 16
G_TAU = 16.0
G_CHUNK = 64
N_BUCKETS = 32
MAX_DIST = 2048
CONV_W = 3
LN_EPS = 1e-5
ALPHA = (2 * DEPTH) ** 0.25
NEG = -1e30
LANES = 128
SUBLANES = 8
MM_ROWS = 1024
VMEM_LIMIT = 56 << 20


def _params(*sem):
    return pltpu.CompilerParams(dimension_semantics=sem, vmem_limit_bytes=VMEM_LIMIT)


def _resident(shape, index_map):
    return pl.BlockSpec(shape, index_map, pipeline_mode=pl.Buffered(1))


def _nt(a, b):
    return lax.dot_general(a, b, (((1,), (1,)), ((), ())), preferred_element_type=F32)


def _tn(a, b):
    return lax.dot_general(a, b, (((0,), (0,)), ((), ())), preferred_element_type=F32)


def _dot(a, b):
    return jnp.dot(a, b, preferred_element_type=F32)


def _sigmoid(x):
    return 0.5 * jnp.tanh(0.5 * x) + 0.5


def _split2(a):
    hi = a.astype(BF16)
    lo = (a - hi.astype(F32)).astype(BF16)
    return hi, lo


def _layernorm(z, g, b):
    mu = jnp.mean(z, axis=-1, keepdims=True)
    zc = z - mu
    var = jnp.mean(zc * zc, axis=-1, keepdims=True)
    return zc * lax.rsqrt(var + LN_EPS) * g + b


def _rows(start, size, stride):
    return pl.ds(start, size) if stride == 1 else pl.ds(start, size, stride=stride)


def _proj_kernel(*refs, d, act, shift, sub):
    refs = list(refs)
    x_ref, w_ref = refs.pop(0), refs.pop(0)
    wn_ref = refs.pop(0) if shift else None
    o_ref, wb_ref = refs.pop(0), refs.pop(0)

    @pl.when(pl.program_id(1) == 0)
    def _():
        w = w_ref[0]
        if shift:
            tn = w.shape[0]
            w = jnp.concatenate([w, wn_ref[0]], axis=0)[shift:shift + tn]
        wb_ref[...] = w.astype(BF16)

    acc_ref = refs.pop(0) if d > 1 else None
    for part in range(x_ref.shape[0] // sub):
        acc = _nt(x_ref[part * sub:(part + 1) * sub, :], wb_ref[...])
        if act == "sigmoid":
            acc = _sigmoid(acc)
        if d == 1:
            o_ref[0, part * sub:(part + 1) * sub, :] = acc.astype(o_ref.dtype)
            continue
        n = sub // d
        pitch = acc_ref.shape[1] // n
        for c in range(acc.shape[1] // LANES):
            cs = slice(c * LANES, (c + 1) * LANES)
            if pitch == d:
                acc_ref[c] = acc[:, cs]
            else:
                for g in range(n):
                    acc_ref[c, g * pitch:g * pitch + d, :] = acc[g * d:(g + 1) * d, cs]
            for r in range(d):
                o_ref[r, part * n:(part + 1) * n, cs] = (
                    acc_ref[c, pl.ds(r, n, stride=pitch), :].astype(o_ref.dtype))


def _proj(xb, w_t, layer, col0, ncols, d, out_dtype, *, act=None, tm=2048, tn=1024):
    s, dm = xb.shape
    tm, tn = min(tm, s), min(tn, ncols)
    sub = min(tm, MM_ROWS)
    shift = col0 % LANES
    c0 = (col0 - shift) // tn
    assert col0 - shift == c0 * tn and ncols % tn == 0 and s % tm == 0 and tm % sub == 0
    assert sub % (2 * SUBLANES * d) == 0 and shift % SUBLANES == 0
    in_specs = [pl.BlockSpec((tm, dm), lambda j, m: (m, 0)),
                pl.BlockSpec((1, tn, dm), lambda j, m: (layer, c0 + j, 0))]
    operands = [xb, w_t]
    if shift:
        per = tn // LANES
        in_specs.append(pl.BlockSpec((1, LANES, dm), lambda j, m: (layer, (c0 + j + 1) * per, 0)))
        operands.append(w_t)
    pitch = d + SUBLANES if d % (2 * SUBLANES) == 0 else d
    scratch = [pltpu.VMEM((tn, dm), BF16)]
    if d > 1:
        scratch.append(pltpu.VMEM((tn // LANES, sub // d * pitch, LANES), F32))
    return pl.pallas_call(
        functools.partial(_proj_kernel, d=d, act=act, shift=shift, sub=sub),
        grid=(ncols // tn, s // tm),
        in_specs=in_specs,
        out_specs=pl.BlockSpec((d, tm // d, tn), lambda j, m: (0, m, j)),
        out_shape=jax.ShapeDtypeStruct((d, s // d, ncols), out_dtype),
        scratch_shapes=scratch,
        compiler_params=_params("arbitrary", "arbitrary"),
        name=f"proj_d{d}_{ncols}",
    )(*operands)


def _gate_kernel(x_ref, wl_ref, gw_ref, gb_ref, cum_ref, *, n_chunks):
    g = _nt(x_ref[...], wl_ref[0].astype(BF16))
    g_hi, g_lo = _split2(g)
    w_hi, w_lo = _split2(gw_ref[...])
    z = _dot(g_hi, w_hi) + _dot(g_hi, w_lo) + _dot(g_lo, w_hi) + gb_ref[...]
    log_a = (jnp.minimum(z, 0.0) - jnp.log(1.0 + jnp.exp(-jnp.abs(z)))) * (1.0 / G_TAU)
    row = lax.broadcasted_iota(jnp.int32, (G_CHUNK, G_CHUNK), 0)
    col = lax.broadcasted_iota(jnp.int32, (G_CHUNK, G_CHUNK), 1)
    tril = jnp.where(row >= col, 1.0, 0.0).astype(BF16)
    for c in range(n_chunks):
        hi, lo = _split2(log_a[c * G_CHUNK:(c + 1) * G_CHUNK])
        cum_ref[c * G_CHUNK:(c + 1) * G_CHUNK, :] = _dot(tril, hi) + _dot(tril, lo)


def _gate(xb, w_t, layer, low0, gate_w, gate_b, tm=1024):
    s, dm = xb.shape
    dk = gate_w.shape[1]
    tm = min(tm, s)
    assert low0 % LANES == 0
    return pl.pallas_call(
        functools.partial(_gate_kernel, n_chunks=tm // G_CHUNK),
        grid=(s // tm,),
        in_specs=[pl.BlockSpec((tm, dm), lambda m: (m, 0)),
                  pl.BlockSpec((1, LANES, dm), lambda m: (layer, low0 // LANES, 0)),
                  pl.BlockSpec((LANES, dk), lambda m: (0, 0)),
                  pl.BlockSpec((1, dk), lambda m: (0, 0))],
        out_specs=pl.BlockSpec((tm, dk), lambda m: (m, 0)),
        out_shape=jax.ShapeDtypeStruct((s, dk), F32),
        compiler_params=_params("parallel"),
        name="gla_gate",
    )(xb, w_t, gate_w, gate_b)


def _gla_kernel(q_ref, k_ref, v_ref, c_ref, r_ref, ng_ref, o_ref, st_ref, *, n_chunks, scale):
    @pl.when(pl.program_id(1) == 0)
    def _():
        st_ref[...] = jnp.zeros_like(st_ref)

    row = lax.broadcasted_iota(jnp.int32, (G_CHUNK, G_CHUNK), 0)
    col = lax.broadcasted_iota(jnp.int32, (G_CHUNK, G_CHUNK), 1)
    causal = row >= col
    ng = ng_ref[...]
    for c in range(n_chunks):
        sl = slice(c * G_CHUNK, (c + 1) * G_CHUNK)
        cum = c_ref[sl, :]
        last = cum[G_CHUNK - 1:G_CHUNK, :]
        q = q_ref[0, sl, :].astype(F32) * scale
        k = k_ref[0, sl, :].astype(F32)
        v = v_ref[0, sl, :].astype(BF16)
        qd = (q * jnp.exp(cum)).astype(BF16)
        ki = (k * jnp.exp(-cum)).astype(BF16)
        ke = (k * jnp.exp(last - cum)).astype(BF16)
        att = jnp.where(causal, _nt(qd, ki), 0.0)
        st = st_ref[...]
        o = _dot(att.astype(BF16), v) + _nt(qd, st.astype(BF16))
        st_ref[...] = st * jnp.exp(last) + _tn(v, ke)
        ms = jnp.mean(o * o, axis=-1, keepdims=True)
        on = o * lax.rsqrt(ms + LN_EPS) * ng
        gr = r_ref[0, sl, :].astype(F32)
        o_ref[sl, :] = (on * (gr * _sigmoid(gr))).astype(o_ref.dtype)


def _gla(gla_in, cum, norm_g, tg=2048):
    s, dk = cum.shape
    dv = (gla_in.shape[2] - 2 * dk) // 2
    hk, hv = dk // G_HEADS, dv // G_HEADS
    tg = min(tg, s)
    kq = dk // hk
    kv = 2 * dk // hv
    kr = (2 * dk + dv) // hv
    return pl.pallas_call(
        functools.partial(_gla_kernel, n_chunks=tg // G_CHUNK, scale=hk ** -0.5),
        grid=(G_HEADS, s // tg),
        in_specs=[pl.BlockSpec((1, tg, hk), lambda h, j: (0, j, h)),
                  pl.BlockSpec((1, tg, hk), lambda h, j: (0, j, kq + h)),
                  pl.BlockSpec((1, tg, hv), lambda h, j: (0, j, kv + h)),
                  pl.BlockSpec((tg, hk), lambda h, j: (j, h)),
                  pl.BlockSpec((1, tg, hv), lambda h, j: (0, j, kr + h)),
                  pl.BlockSpec((1, hv), lambda h, j: (0, 0))],
        out_specs=pl.BlockSpec((tg, hv), lambda h, j: (j, h)),
        out_shape=jax.ShapeDtypeStruct((s, dv), BF16),
        scratch_shapes=[pltpu.VMEM((hv, hk), F32)],
        compiler_params=_params("parallel", "arbitrary"),
        name="gla",
    )(gla_in, gla_in, gla_in, cum, gla_in, norm_g)


def _attn_kernel(q_ref, kc_ref, vc_ref, kp_ref, vp_ref, b_ref, o_ref, lse_ref, *, d, rpg, nblk, scale):
    first = pl.program_id(0) == 0
    r0 = pl.program_id(1) * rpg if rpg < d else 0
    head_row = lax.broadcasted_iota(jnp.int32, (A_HEADS, A_BLOCK), 0)
    pad = jnp.zeros((LANES - A_HEADS, A_BLOCK), F32)
    for rl in range(rpg):
        for b in range(nblk):
            rows = slice(b * A_BLOCK, (b + 1) * A_BLOCK)
            prev = slice((b - 1) * A_BLOCK, b * A_BLOCK)
            pos = _rows(b * A_BLOCK * d + r0 + rl, A_BLOCK, d)
            lse_t = jnp.zeros((A_HEADS, A_BLOCK), F32)
            for h in range(A_HEADS):
                cs = slice(h * A_HEAD_DIM, (h + 1) * A_HEAD_DIM)
                q = q_ref[rl, rows, cs]
                if b == 0:
                    kp, vp = kp_ref[rl, :, cs], vp_ref[rl, :, cs]
                else:
                    kp, vp = kc_ref[rl, prev, cs], vc_ref[rl, prev, cs]
                kk = jnp.concatenate([kp, kc_ref[rl, rows, cs]], axis=0)
                vv = jnp.concatenate([vp, vc_ref[rl, rows, cs]], axis=0)
                sc = _nt(kk, q) * scale + b_ref[h]
                if b == 0:
                    sc = jnp.concatenate([jnp.where(first, NEG, sc[:A_BLOCK]), sc[A_BLOCK:]], axis=0)
                m = jnp.max(sc, axis=0, keepdims=True)
                p = jnp.exp(sc - m)
                l = jnp.sum(p, axis=0, keepdims=True)
                o_t = _tn(vv, p.astype(BF16)) * (1.0 / l)
                o_ref[h, pos, :] = o_t.T
                lse_t = jnp.where(head_row == h, m + jnp.log(l), lse_t)
            lse_ref[pos, :] = jnp.concatenate([lse_t, pad], axis=0).T


def _attention(qkv, bias, group):
    d, n_sub, _ = qkv.shape
    s = d * n_sub
    rpg = min(d, A_STEP_BLOCKS)
    nblk = max(1, min(A_STEP_BLOCKS // rpg, n_sub // A_BLOCK))
    tq = nblk * A_BLOCK
    cur = lambda c: pl.BlockSpec((rpg, tq, A_WIDTH), lambda j, g: (g, j, c))
    prv = lambda c: pl.BlockSpec((rpg, A_BLOCK, A_WIDTH),
                                 lambda j, g: (g, jnp.maximum(j * nblk - 1, 0), c))
    return pl.pallas_call(
        functools.partial(_attn_kernel, d=d, rpg=rpg, nblk=nblk, scale=A_HEAD_DIM ** -0.5),
        grid=(n_sub // tq, d // rpg),
        in_specs=[cur(0), cur(1), cur(2), prv(1), prv(2),
                  _resident((A_HEADS, 2 * A_BLOCK, A_BLOCK), lambda j, g: (group, 0, 0))],
        out_specs=[pl.BlockSpec((A_HEADS, d * tq, A_HEAD_DIM), lambda j, g: (0, j, 0)),
                   pl.BlockSpec((d * tq, LANES), lambda j, g: (j, 0))],
        out_shape=[jax.ShapeDtypeStruct((A_HEADS, s, A_HEAD_DIM), F32),
                   jax.ShapeDtypeStruct((s, LANES), F32)],
        compiler_params=_params("parallel", "arbitrary"),
        name=f"attn_d{d}",
    )(qkv, qkv, qkv, qkv, qkv, bias)


def _combine_kernel(o0_ref, o1_ref, o2_ref, l0_ref, l1_ref, l2_ref, w_ref, y_ref):
    l0, l1, l2 = l0_ref[...], l1_ref[...], l2_ref[...]
    m = jnp.maximum(jnp.maximum(l0, l1), l2)
    e0, e1, e2 = jnp.exp(l0 - m), jnp.exp(l1 - m), jnp.exp(l2 - m)
    inv = 1.0 / (e0 + e1 + e2)
    w0, w1, w2 = e0 * inv, e1 * inv, e2 * inv
    pieces = []
    for h in range(A_HEADS):
        ya = w0[:, h:h + 1] * o0_ref[h] + w1[:, h:h + 1] * o1_ref[h] + w2[:, h:h + 1] * o2_ref[h]
        pieces.append(ya.astype(BF16))
    y_ref[...] = _dot(jnp.concatenate(pieces, axis=1), w_ref[0]).astype(y_ref.dtype)


def _combine(outs, lses, w_proj_a, layer, tm=512):
    s = outs[0].shape[1]
    dm = w_proj_a.shape[2]
    tm = min(tm, s)
    o_spec = pl.BlockSpec((A_HEADS, tm, A_HEAD_DIM), lambda m: (0, m, 0))
    l_spec = pl.BlockSpec((tm, LANES), lambda m: (m, 0))
    return pl.pallas_call(
        _combine_kernel,
        grid=(s // tm,),
        in_specs=[o_spec, o_spec, o_spec, l_spec, l_spec, l_spec,
                  _resident((1, A_WIDTH, dm), lambda m: (layer, 0, 0))],
        out_specs=pl.BlockSpec((tm, dm), lambda m: (m, 0)),
        out_shape=jax.ShapeDtypeStruct((s, dm), BF16),
        compiler_params=_params("parallel"),
        name="combine_proj_a",
    )(*outs, *lses, w_proj_a)


def _merge_kernel(a_ref, w_ref, ya_ref, sa_ref, sb_ref, y_ref, wb_ref):
    @pl.when(pl.program_id(1) == 0)
    def _():
        wb_ref[...] = w_ref[0].astype(BF16)

    yb = _dot(a_ref[...], wb_ref[...])
    y = sa_ref[0].astype(F32) * ya_ref[...].astype(F32) + sb_ref[0].astype(F32) * yb
    y_ref[...] = y.astype(y_ref.dtype)


def _merge(a, w_proj_b, layer, ya, gates, tm=1024, tn=1024):
    s, dv = a.shape
    dm = w_proj_b.shape[2]
    tm, tn = min(tm, s), min(tn, dm)
    nb = dm // tn
    return pl.pallas_call(
        _merge_kernel,
        grid=(nb, s // tm),
        in_specs=[pl.BlockSpec((tm, dv), lambda j, m: (m, 0)),
                  pl.BlockSpec((1, dv, tn), lambda j, m: (layer, 0, j)),
                  pl.BlockSpec((tm, tn), lambda j, m: (m, j)),
                  pl.BlockSpec((1, tm, tn), lambda j, m: (0, m, j)),
                  pl.BlockSpec((1, tm, tn), lambda j, m: (0, m, nb + j))],
        out_specs=pl.BlockSpec((tm, tn), lambda j, m: (m, j)),
        out_shape=jax.ShapeDtypeStruct((s, dm), BF16),
        scratch_shapes=[pltpu.VMEM((dv, tn), BF16)],
        compiler_params=_params("arbitrary", "arbitrary"),
        name="proj_b_merge",
    )(a, w_proj_b, ya, gates, gates)


def _out_ln_kernel(y_ref, w_ref, x_ref, g_ref, b_ref, xo_ref, xb_ref):
    z = ALPHA * x_ref[...] + _dot(y_ref[...], w_ref[0])
    xn = _layernorm(z, g_ref[...], b_ref[...])
    xo_ref[...] = xn
    xb_ref[...] = xn.astype(BF16)


def _out_ln(y, w_out, layer, x, g, b, tm=512):
    s, dm = x.shape
    tm = min(tm, s)
    row = pl.BlockSpec((tm, dm), lambda m: (m, 0))
    vec = pl.BlockSpec((1, dm), lambda m: (0, 0))
    return pl.pallas_call(
        _out_ln_kernel,
        grid=(s // tm,),
        in_specs=[row, _resident((1, dm, dm), lambda m: (layer, 0, 0)), row, vec, vec],
        out_specs=[row, row],
        out_shape=[jax.ShapeDtypeStruct((s, dm), F32), jax.ShapeDtypeStruct((s, dm), BF16)],
        compiler_params=_params("parallel"),
        name="out_proj_ln",
    )(y, w_out, x, g, b)


def _ffn1_kernel(x_ref, wg_ref, wu_ref, cw_ref, cb_ref, h_ref, wgb_ref, wub_ref, halo_ref):
    tm = x_ref.shape[0]

    @pl.when(pl.program_id(1) == 0)
    def _():
        wgb_ref[...] = wg_ref[0].astype(BF16)
        wub_ref[...] = wu_ref[0].astype(BF16)
        halo_ref[...] = jnp.zeros_like(halo_ref)

    cw = cw_ref[0]
    sub = min(tm, MM_ROWS)
    for part in range(tm // sub):
        rows = slice(part * sub, (part + 1) * sub)
        x = x_ref[rows, :]
        g = _dot(x, wgb_ref[...])
        u = _dot(x, wub_ref[...])
        halo = halo_ref[...]
        row = lax.broadcasted_iota(jnp.int32, g.shape, 0)
        g1 = jnp.where(row == 0, halo[7:8], pltpu.roll(g, 1, axis=0))
        g2 = jnp.where(row == 0, halo[6:7], jnp.where(row == 1, halo[7:8], pltpu.roll(g, 2, axis=0)))
        halo_ref[...] = g[sub - 8:sub]
        gc = cb_ref[0] + cw[0:1] * g2 + cw[1:2] * g1 + cw[2:3] * g
        h_ref[rows, :] = (gc * _sigmoid(gc) * u).astype(h_ref.dtype)


def _ffn1(xb, w_gate, w_up, conv_w, conv_b, layer, tm=2048, tn=512):
    s, dm = xb.shape
    dff = w_gate.shape[2]
    tm = min(tm, s)
    wspec = pl.BlockSpec((1, dm, tn), lambda j, m: (layer, 0, j))
    return pl.pallas_call(
        _ffn1_kernel,
        grid=(dff // tn, s // tm),
        in_specs=[pl.BlockSpec((tm, dm), lambda j, m: (m, 0)), wspec, wspec,
                  pl.BlockSpec((1, CONV_W, tn), lambda j, m: (layer, 0, j)),
                  pl.BlockSpec((1, 1, tn), lambda j, m: (layer, 0, j))],
        out_specs=pl.BlockSpec((tm, tn), lambda j, m: (m, j)),
        out_shape=jax.ShapeDtypeStruct((s, dff), BF16),
        scratch_shapes=[pltpu.VMEM((dm, tn), BF16), pltpu.VMEM((dm, tn), BF16),
                        pltpu.VMEM((8, tn), F32)],
        compiler_params=_params("arbitrary", "arbitrary"),
        name="ffn_gate_up_conv",
    )(xb, w_gate, w_up, conv_w, conv_b)


def _ffn2_kernel(h_ref, w_ref, x_ref, g_ref, b_ref, xo_ref, xb_ref):
    k = pl.program_id(1)

    @pl.when(k == 0)
    def _():
        xo_ref[...] = ALPHA * x_ref[...]

    slab = 512
    for c in range(xo_ref.shape[1] // slab):
        cs = slice(c * slab, (c + 1) * slab)
        xo_ref[:, cs] += _dot(h_ref[...], w_ref[0, :, cs])

    @pl.when(k == pl.num_programs(1) - 1)
    def _():
        xn = _layernorm(xo_ref[...], g_ref[...], b_ref[...])
        xo_ref[...] = xn
        xb_ref[...] = xn.astype(BF16)


def _ffn2(h, w_down, layer, x, g, b, tm=1024, tk=512):
    s, dm = x.shape
    dff = h.shape[1]
    tm = min(tm, s)
    row = pl.BlockSpec((tm, dm), lambda m, k: (m, 0))
    vec = pl.BlockSpec((1, dm), lambda m, k: (0, 0))
    return pl.pallas_call(
        _ffn2_kernel,
        grid=(s // tm, dff // tk),
        in_specs=[pl.BlockSpec((tm, tk), lambda m, k: (m, k)),
                  pl.BlockSpec((1, tk, dm), lambda m, k: (layer, k, 0)), row, vec, vec],
        out_specs=[row, row],
        out_shape=[jax.ShapeDtypeStruct((s, dm), F32), jax.ShapeDtypeStruct((s, dm), BF16)],
        compiler_params=_params("parallel", "arbitrary"),
        name="ffn_down_ln",
    )(h, w_down, x, g, b)


def _band_buckets(dilation):
    qi = jnp.arange(A_BLOCK)[None, :]
    ci = jnp.arange(2 * A_BLOCK)[:, None]
    off = A_BLOCK + qi - ci
    valid = (off >= 0) & (off <= A_SPAN)
    dist = dilation * jnp.clip(off, 0, A_SPAN)
    max_exact = N_BUCKETS // 2
    df = jnp.maximum(dist, 1).astype(F32)
    large = max_exact + (jnp.log(df / max_exact) / math.log(MAX_DIST / max_exact)
                         * (N_BUCKETS - max_exact)).astype(jnp.int32)
    bucket = jnp.where(dist < max_exact, dist, jnp.minimum(large, N_BUCKETS - 1))
    return jnp.where(valid, bucket, -1).astype(jnp.int32)


def _bias_kernel(tab_ref, idx_ref, o_ref):
    g = pl.program_id(0)
    idx = idx_ref[0]
    for h in range(A_HEADS):
        acc = jnp.full(idx.shape, NEG, F32)
        for b in range(N_BUCKETS):
            acc = jnp.where(idx == b, tab_ref[b, g * A_HEADS + h], acc)
        o_ref[h] = acc


def _band_bias(rel_bias):
    idx = jnp.stack([_band_buckets(d) for d in DILATIONS])
    n_groups = len(DILATIONS)
    return pl.pallas_call(
        _bias_kernel,
        grid=(n_groups,),
        in_specs=[pl.BlockSpec(memory_space=pltpu.SMEM),
                  pl.BlockSpec((1, 2 * A_BLOCK, A_BLOCK), lambda g: (g, 0, 0))],
        out_specs=pl.BlockSpec((A_HEADS, 2 * A_BLOCK, A_BLOCK), lambda g: (g, 0, 0)),
        out_shape=jax.ShapeDtypeStruct((n_groups * A_HEADS, 2 * A_BLOCK, A_BLOCK), F32),
        compiler_params=_params("parallel"),
        name="band_bias",
    )(rel_bias, idx)


def kernel(x, w_in, gla_gate_w, gla_gate_b, gla_norm_g, w_proj_a, w_proj_b, w_out, rel_bias,
           ln1_g, ln1_b, ffn_w_gate, ffn_w_up, ffn_conv_w, ffn_conv_b, ffn_w_down, ln2_g, ln2_b):
    _, s, dm = x.shape
    dk = gla_gate_w.shape[2]
    dv = w_proj_b.shape[1]
    n_groups = len(DILATIONS)
    a_cols = 3 * A_WIDTH * n_groups
    gla_cols = 2 * dk + 2 * dv
    low0 = a_cols + gla_cols
    bias = _band_bias(rel_bias)
    w_t = jnp.swapaxes(w_in, 1, 2)

    w_a, w_o, w_d = w_proj_a.astype(BF16), w_out.astype(BF16), ffn_w_down.astype(BF16)

    xf = x.reshape(s, dm)
    xb = xf.astype(BF16)
    for i in range(DEPTH):
        gate_w = jnp.pad(gla_gate_w[i], ((0, LANES - G_RANK), (0, 0)))
        outs, lses = [], []
        for g, d in enumerate(DILATIONS):
            qkv = _proj(xb, w_t, i, 3 * A_WIDTH * g, 3 * A_WIDTH, d, BF16)
            o, lse = _attention(qkv, bias, g)
            outs.append(o)
            lses.append(lse)
        ya = _combine(outs, lses, w_a, i)
        gla_in = _proj(xb, w_t, i, a_cols, gla_cols, 1, F32)
        cum = _gate(xb, w_t, i, low0, gate_w, gla_gate_b[i].reshape(1, dk))
        a = _gla(gla_in, cum, gla_norm_g[i].reshape(1, -1))
        gates = _proj(xb, w_t, i, low0 + G_RANK, 2 * dm, 1, BF16, act="sigmoid")
        y = _merge(a, w_proj_b, i, ya, gates)
        xf, xb = _out_ln(y, w_o, i, xf, ln1_g[i].reshape(1, dm), ln1_b[i].reshape(1, dm))
        h = _ffn1(xb, ffn_w_gate, ffn_w_up, ffn_conv_w, ffn_conv_b.reshape(DEPTH, 1, -1), i)
        xf, xb = _ffn2(h, w_d, i, xf, ln2_g[i].reshape(1, dm), ln2_b[i].reshape(1, dm))
    return xf.reshape(x.shape)
```

```python
import functools
import math

import jax
import jax.numpy as jnp
from jax import lax
from jax.experimental import pallas as pl
from jax.experimental.pallas import tpu as pltpu

F32 = jnp.float32
BF16 = jnp.bfloat16

DEPTH = 4
DILATIONS = (1, 4, 16)
A_HEADS = 8
A_HEAD_DIM = 128
A_WIDTH = A_HEADS * A_HEAD_DIM
A_SPAN = 128
A_BLOCK = 128
A_STEP_BLOCKS = 4
G_HEADS = 4
G_RANK = 16
G_TAU = 16.0
G_CHUNK = 64
N_BUCKETS = 32
MAX_DIST = 2048
CONV_W = 3
LN_EPS = 1e-5
ALPHA = (2 * DEPTH) ** 0.25
NEG = -1e30
LANES = 128
SUBLANES = 8
MM_ROWS = 1024
VMEM_LIMIT = 56 << 20


def _params(*sem):
    return pltpu.CompilerParams(dimension_semantics=sem, vmem_limit_bytes=VMEM_LIMIT)


def _resident(shape, index_map):
    return pl.BlockSpec(shape, index_map, pipeline_mode=pl.Buffered(1))


def _nt(a, b):
    return lax.dot_general(a, b, (((1,), (1,)), ((), ())), preferred_element_type=F32)


def _tn(a, b):
    return lax.dot_general(a, b, (((0,), (0,)), ((), ())), preferred_element_type=F32)


def _dot(a, b):
    return jnp.dot(a, b, preferred_element_type=F32)


def _sigmoid(x):
    return 0.5 * jnp.tanh(0.5 * x) + 0.5


def _split2(a):
    hi = a.astype(BF16)
    lo = (a - hi.astype(F32)).astype(BF16)
    return hi, lo


def _layernorm(z, g, b):
    mu = jnp.mean(z, axis=-1, keepdims=True)
    zc = z - mu
    var = jnp.mean(zc * zc, axis=-1, keepdims=True)
    return zc * lax.rsqrt(var + LN_EPS) * g + b


def _rows(start, size, stride):
    return pl.ds(start, size) if stride == 1 else pl.ds(start, size, stride=stride)


def _proj_kernel(*refs, d, act, shift, sub):
    refs = list(refs)
    x_ref, w_ref = refs.pop(0), refs.pop(0)
    wn_ref = refs.pop(0) if shift else None
    o_ref, wb_ref = refs.pop(0), refs.pop(0)

    @pl.when(pl.program_id(1) == 0)
    def _():
        w = w_ref[0]
        if shift:
            tn = w.shape[0]
            w = jnp.concatenate([w, wn_ref[0]], axis=0)[shift:shift + tn]
        wb_ref[...] = w.astype(BF16)

    acc_ref = refs.pop(0) if d > 1 else None
    for part in range(x_ref.shape[0] // sub):
        acc = _nt(x_ref[part * sub:(part + 1) * sub, :], wb_ref[...])
        if act == "sigmoid":
            acc = _sigmoid(acc)
        if d == 1:
            o_ref[0, part * sub:(part + 1) * sub, :] = acc.astype(o_ref.dtype)
            continue
        n = sub // d
        pitch = acc_ref.shape[1] // n
        for c in range(acc.shape[1] // LANES):
            cs = slice(c * LANES, (c + 1) * LANES)
            if pitch == d:
                acc_ref[c] = acc[:, cs]
            else:
                for g in range(n):
                    acc_ref[c, g * pitch:g * pitch + d, :] = acc[g * d:(g + 1) * d, cs]
            for r in range(d):
                o_ref[r, part * n:(part + 1) * n, cs] = (
                    acc_ref[c, pl.ds(r, n, stride=pitch), :].astype(o_ref.dtype))


def _proj(xb, w_t, layer, col0, ncols, d, out_dtype, *, act=None, tm=2048, tn=1024):
    s, dm = xb.shape
    tm, tn = min(tm, s), min(tn, ncols)
    sub = min(tm, MM_ROWS)
    shift = col0 % LANES
    c0 = (col0 - shift) // tn
    assert col0 - shift == c0 * tn and ncols % tn == 0 and s % tm == 0 and tm % sub == 0
    assert sub % (2 * SUBLANES * d) == 0 and shift % SUBLANES == 0
    in_specs = [pl.BlockSpec((tm, dm), lambda j, m: (m, 0)),
                pl.BlockSpec((1, tn, dm), lambda j, m: (layer, c0 + j, 0))]
    operands = [xb, w_t]
    if shift:
        per = tn // LANES
        in_specs.append(pl.BlockSpec((1, LANES, dm), lambda j, m: (layer, (c0 + j + 1) * per, 0)))
        operands.append(w_t)
    pitch = d + SUBLANES if d % (2 * SUBLANES) == 0 else d
    scratch = [pltpu.VMEM((tn, dm), BF16)]
    if d > 1:
        scratch.append(pltpu.VMEM((tn // LANES, sub // d * pitch, LANES), F32))
    return pl.pallas_call(
        functools.partial(_proj_kernel, d=d, act=act, shift=shift, sub=sub),
        grid=(ncols // tn, s // tm),
        in_specs=in_specs,
        out_specs=pl.BlockSpec((d, tm // d, tn), lambda j, m: (0, m, j)),
        out_shape=jax.ShapeDtypeStruct((d, s // d, ncols), out_dtype),
        scratch_shapes=scratch,
        compiler_params=_params("arbitrary", "arbitrary"),
        name=f"proj_d{d}_{ncols}",
    )(*operands)


def _gate_kernel(x_ref, wl_ref, gw_ref, gb_ref, cum_ref, *, n_chunks):
    g = _nt(x_ref[...], wl_ref[0].astype(BF16))
    g_hi, g_lo = _split2(g)
    w_hi, w_lo = _split2(gw_ref[...])
    z = _dot(g_hi, w_hi) + _dot(g_hi, w_lo) + _dot(g_lo, w_hi) + gb_ref[...]
    log_a = (jnp.minimum(z, 0.0) - jnp.log(1.0 + jnp.exp(-jnp.abs(z)))) * (1.0 / G_TAU)
    row = lax.broadcasted_iota(jnp.int32, (G_CHUNK, G_CHUNK), 0)
    col = lax.broadcasted_iota(jnp.int32, (G_CHUNK, G_CHUNK), 1)
    tril = jnp.where(row >= col, 1.0, 0.0).astype(BF16)
    for c in range(n_chunks):
        hi, lo = _split2(log_a[c * G_CHUNK:(c + 1) * G_CHUNK])
        cum_ref[c * G_CHUNK:(c + 1) * G_CHUNK, :] = _dot(tril, hi) + _dot(tril, lo)


def _gate(xb, w_t, layer, low0, gate_w, gate_b, tm=1024):
    s, dm = xb.shape
    dk = gate_w.shape[1]
    tm = min(tm, s)
    assert low0 % LANES == 0
    return pl.pallas_call(
        functools.partial(_gate_kernel, n_chunks=tm // G_CHUNK),
        grid=(s // tm,),
        in_specs=[pl.BlockSpec((tm, dm), lambda m: (m, 0)),
                  pl.BlockSpec((1, LANES, dm), lambda m: (layer, low0 // LANES, 0)),
                  pl.BlockSpec((LANES, dk), lambda m: (0, 0)),
                  pl.BlockSpec((1, dk), lambda m: (0, 0))],
        out_specs=pl.BlockSpec((tm, dk), lambda m: (m, 0)),
        out_shape=jax.ShapeDtypeStruct((s, dk), F32),
        compiler_params=_params("parallel"),
        name="gla_gate",
    )(xb, w_t, gate_w, gate_b)


def _gla_kernel(q_ref, k_ref, v_ref, c_ref, r_ref, ng_ref, o_ref, st_ref, *, n_pairs, scale):
    @pl.when(pl.program_id(1) == 0)
    def _():
        st_ref[...] = jnp.zeros_like(st_ref)

    c = G_CHUNK
    row = lax.broadcasted_iota(jnp.int32, (c, c), 0)
    col = lax.broadcasted_iota(jnp.int32, (c, c), 1)
    causal = row >= col
    ng = ng_ref[...]
    for pr in range(n_pairs):
        ra = slice(2 * pr * c, (2 * pr + 1) * c)
        rb = slice((2 * pr + 1) * c, (2 * pr + 2) * c)
        rab = slice(2 * pr * c, (2 * pr + 2) * c)
        cum_a, cum_b = c_ref[ra, :], c_ref[rb, :]
        last_a, last_b = cum_a[c - 1:c, :], cum_b[c - 1:c, :]
        k_a = k_ref[0, ra, :].astype(F32)
        k_b = k_ref[0, rb, :].astype(F32)
        v = v_ref[0, rab, :].astype(BF16)
        qd_a = q_ref[0, ra, :].astype(F32) * scale * jnp.exp(cum_a)
        qd_b = q_ref[0, rb, :].astype(F32) * scale * jnp.exp(cum_b)
        ke_a = k_a * jnp.exp(last_a - cum_a)
        ke_b = k_b * jnp.exp(last_b - cum_b)
        qd_a16, qd_b16 = qd_a.astype(BF16), qd_b.astype(BF16)
        att_aa = jnp.where(causal, _nt(qd_a16, (k_a * jnp.exp(-cum_a)).astype(BF16)), 0.0)
        att_bb = jnp.where(causal, _nt(qd_b16, (k_b * jnp.exp(-cum_b)).astype(BF16)), 0.0)
        att_ba = _nt(qd_b16, ke_a.astype(BF16))
        q_st = jnp.concatenate([qd_a16, (qd_b * jnp.exp(last_a)).astype(BF16)], axis=0)
        k_st = jnp.concatenate([(ke_a * jnp.exp(last_b)).astype(BF16), ke_b.astype(BF16)], axis=0)
        st = st_ref[...]
        o = _nt(q_st, st.astype(BF16)) + _dot(jnp.concatenate([att_aa, att_ba], axis=0).astype(BF16), v[:c])
        o = jnp.concatenate([o[:c], o[c:] + _dot(att_bb.astype(BF16), v[c:])], axis=0)
        st_ref[...] = st * jnp.exp(last_a + last_b) + _tn(v, k_st)
        ms = jnp.mean(o * o, axis=-1, keepdims=True)
        on = o * lax.rsqrt(ms + LN_EPS) * ng
        gr = r_ref[0, rab, :].astype(F32)
        o_ref[rab, :] = (on * (gr * _sigmoid(gr))).astype(o_ref.dtype)


def _gla(gla_in, cum, norm_g, tg=2048):
    s, dk = cum.shape
    dv = (gla_in.shape[2] - 2 * dk) // 2
    hk, hv = dk // G_HEADS, dv // G_HEADS
    tg = min(tg, s)
    kq = dk // hk
    kv = 2 * dk // hv
    kr = (2 * dk + dv) // hv
    return pl.pallas_call(
        functools.partial(_gla_kernel, n_pairs=tg // (2 * G_CHUNK), scale=hk ** -0.5),
        grid=(G_HEADS, s // tg),
        in_specs=[pl.BlockSpec((1, tg, hk), lambda h, j: (0, j, h)),
                  pl.BlockSpec((1, tg, hk), lambda h, j: (0, j, kq + h)),
                  pl.BlockSpec((1, tg, hv), lambda h, j: (0, j, kv + h)),
                  pl.BlockSpec((tg, hk), lambda h, j: (j, h)),
                  pl.BlockSpec((1, tg, hv), lambda h, j: (0, j, kr + h)),
                  pl.BlockSpec((1, hv), lambda h, j: (0, 0))],
        out_specs=pl.BlockSpec((tg, hv), lambda h, j: (j, h)),
        out_shape=jax.ShapeDtypeStruct((s, dv), BF16),
        scratch_shapes=[pltpu.VMEM((hv, hk), F32)],
        compiler_params=_params("parallel", "arbitrary"),
        name="gla",
    )(gla_in, gla_in, gla_in, cum, gla_in, norm_g)


def _attn_kernel(q_ref, kc_ref, vc_ref, kp_ref, vp_ref, b_ref, o_ref, lse_ref, *, d, rpg, nblk, scale):
    first = pl.program_id(0) == 0
    r0 = pl.program_id(1) * rpg if rpg < d else 0
    head_row = lax.broadcasted_iota(jnp.int32, (A_HEADS, A_BLOCK), 0)
    pad = jnp.zeros((LANES - A_HEADS, A_BLOCK), F32)
    for rl in range(rpg):
        for b in range(nblk):
            rows = slice(b * A_BLOCK, (b + 1) * A_BLOCK)
            prev = slice((b - 1) * A_BLOCK, b * A_BLOCK)
            pos = _rows(b * A_BLOCK * d + r0 + rl, A_BLOCK, d)
            lse_t = jnp.zeros((A_HEADS, A_BLOCK), F32)
            for h in range(A_HEADS):
                cs = slice(h * A_HEAD_DIM, (h + 1) * A_HEAD_DIM)
                q = q_ref[rl, rows, cs]
                if b == 0:
                    kp, vp = kp_ref[rl, :, cs], vp_ref[rl, :, cs]
                else:
                    kp, vp = kc_ref[rl, prev, cs], vc_ref[rl, prev, cs]
                kk = jnp.concatenate([kp, kc_ref[rl, rows, cs]], axis=0)
                vv = jnp.concatenate([vp, vc_ref[rl, rows, cs]], axis=0)
                sc = _nt(kk, q) * scale + b_ref[h]
                if b == 0:
                    sc = jnp.concatenate([jnp.where(first, NEG, sc[:A_BLOCK]), sc[A_BLOCK:]], axis=0)
                m = jnp.max(sc, axis=0, keepdims=True)
                p = jnp.exp(sc - m)
                l = jnp.sum(p, axis=0, keepdims=True)
                o_t = _tn(vv, p.astype(BF16)) * (1.0 / l)
                o_ref[h, pos, :] = o_t.T
                lse_t = jnp.where(head_row == h, m + jnp.log(l), lse_t)
            lse_ref[pos, :] = jnp.concatenate([lse_t, pad], axis=0).T


def _attention(qkv, bias, group):
    d, n_sub, _ = qkv.shape
    s = d * n_sub
    rpg = min(d, A_STEP_BLOCKS)
    nblk = max(1, min(A_STEP_BLOCKS // rpg, n_sub // A_BLOCK))
    tq = nblk * A_BLOCK
    cur = lambda c: pl.BlockSpec((rpg, tq, A_WIDTH), lambda j, g: (g, j, c))
    prv = lambda c: pl.BlockSpec((rpg, A_BLOCK, A_WIDTH),
                                 lambda j, g: (g, jnp.maximum(j * nblk - 1, 0), c))
    return pl.pallas_call(
        functools.partial(_attn_kernel, d=d, rpg=rpg, nblk=nblk, scale=A_HEAD_DIM ** -0.5),
        grid=(n_sub // tq, d // rpg),
        in_specs=[cur(0), cur(1), cur(2), prv(1), prv(2),
                  _resident((A_HEADS, 2 * A_BLOCK, A_BLOCK), lambda j, g: (group, 0, 0))],
        out_specs=[pl.BlockSpec((A_HEADS, d * tq, A_HEAD_DIM), lambda j, g: (0, j, 0)),
                   pl.BlockSpec((d * tq, LANES), lambda j, g: (j, 0))],
        out_shape=[jax.ShapeDtypeStruct((A_HEADS, s, A_HEAD_DIM), F32),
                   jax.ShapeDtypeStruct((s, LANES), F32)],
        compiler_params=_params("parallel", "arbitrary"),
        name=f"attn_d{d}",
    )(qkv, qkv, qkv, qkv, qkv, bias)


def _combine_kernel(o0_ref, o1_ref, o2_ref, l0_ref, l1_ref, l2_ref, w_ref, y_ref):
    l0, l1, l2 = l0_ref[...], l1_ref[...], l2_ref[...]
    m = jnp.maximum(jnp.maximum(l0, l1), l2)
    e0, e1, e2 = jnp.exp(l0 - m), jnp.exp(l1 - m), jnp.exp(l2 - m)
    inv = 1.0 / (e0 + e1 + e2)
    w0, w1, w2 = e0 * inv, e1 * inv, e2 * inv
    pieces = []
    for h in range(A_HEADS):
        ya = w0[:, h:h + 1] * o0_ref[h] + w1[:, h:h + 1] * o1_ref[h] + w2[:, h:h + 1] * o2_ref[h]
        pieces.append(ya.astype(BF16))
    y_ref[...] = _dot(jnp.concatenate(pieces, axis=1), w_ref[0]).astype(y_ref.dtype)


def _combine(outs, lses, w_proj_a, layer, tm=1024):
    s = outs[0].shape[1]
    dm = w_proj_a.shape[2]
    tm = min(tm, s)
    o_spec = pl.BlockSpec((A_HEADS, tm, A_HEAD_DIM), lambda m: (0, m, 0))
    l_spec = pl.BlockSpec((tm, LANES), lambda m: (m, 0))
    return pl.pallas_call(
        _combine_kernel,
        grid=(s // tm,),
        in_specs=[o_spec, o_spec, o_spec, l_spec, l_spec, l_spec,
                  _resident((1, A_WIDTH, dm), lambda m: (layer, 0, 0))],
        out_specs=pl.BlockSpec((tm, dm), lambda m: (m, 0)),
        out_shape=jax.ShapeDtypeStruct((s, dm), BF16),
        compiler_params=_params("parallel"),
        name="combine_proj_a",
    )(*outs, *lses, w_proj_a)


def _merge_kernel(a_ref, w_ref, ya_ref, sa_ref, sb_ref, y_ref, wb_ref):
    @pl.when(pl.program_id(1) == 0)
    def _():
        wb_ref[...] = w_ref[0].astype(BF16)

    yb = _dot(a_ref[...], wb_ref[...])
    y = sa_ref[0].astype(F32) * ya_ref[...].astype(F32) + sb_ref[0].astype(F32) * yb
    y_ref[...] = y.astype(y_ref.dtype)


def _merge(a, w_proj_b, layer, ya, gates, tm=1024, tn=1024):
    s, dv = a.shape
    dm = w_proj_b.shape[2]
    tm, tn = min(tm, s), min(tn, dm)
    nb = dm // tn
    return pl.pallas_call(
        _merge_kernel,
        grid=(nb, s // tm),
        in_specs=[pl.BlockSpec((tm, dv), lambda j, m: (m, 0)),
                  pl.BlockSpec((1, dv, tn), lambda j, m: (layer, 0, j)),
                  pl.BlockSpec((tm, tn), lambda j, m: (m, j)),
                  pl.BlockSpec((1, tm, tn), lambda j, m: (0, m, j)),
                  pl.BlockSpec((1, tm, tn), lambda j, m: (0, m, nb + j))],
        out_specs=pl.BlockSpec((tm, tn), lambda j, m: (m, j)),
        out_shape=jax.ShapeDtypeStruct((s, dm), BF16),
        scratch_shapes=[pltpu.VMEM((dv, tn), BF16)],
        compiler_params=_params("arbitrary", "arbitrary"),
        name="proj_b_merge",
    )(a, w_proj_b, ya, gates, gates)


def _out_ln_kernel(y_ref, w_ref, x_ref, g_ref, b_ref, xo_ref, xb_ref):
    z = ALPHA * x_ref[...] + _dot(y_ref[...], w_ref[0])
    xn = _layernorm(z, g_ref[...], b_ref[...])
    xo_ref[...] = xn
    xb_ref[...] = xn.astype(BF16)


def _out_ln(y, w_out, layer, x, g, b, tm=512):
    s, dm = x.shape
    tm = min(tm, s)
    row = pl.BlockSpec((tm, dm), lambda m: (m, 0))
    vec = pl.BlockSpec((1, dm), lambda m: (0, 0))
    return pl.pallas_call(
        _out_ln_kernel,
        grid=(s // tm,),
        in_specs=[row, _resident((1, dm, dm), lambda m: (layer, 0, 0)), row, vec, vec],
        out_specs=[row, row],
        out_shape=[jax.ShapeDtypeStruct((s, dm), F32), jax.ShapeDtypeStruct((s, dm), BF16)],
        compiler_params=_params("parallel"),
        name="out_proj_ln",
    )(y, w_out, x, g, b)


def _ffn1_kernel(x_ref, wg_ref, wu_ref, cw_ref, cb_ref, h_ref, wgb_ref, wub_ref, halo_ref):
    tm = x_ref.shape[0]

    @pl.when(pl.program_id(1) == 0)
    def _():
        wgb_ref[...] = wg_ref[0].astype(BF16)
        wub_ref[...] = wu_ref[0].astype(BF16)
        halo_ref[...] = jnp.zeros_like(halo_ref)

    cw = cw_ref[0]
    sub = min(tm, MM_ROWS)
    for part in range(tm // sub):
        rows = slice(part * sub, (part + 1) * sub)
        x = x_ref[rows, :]
        g = _dot(x, wgb_ref[...])
        u = _dot(x, wub_ref[...])
        halo = halo_ref[...]
        row = lax.broadcasted_iota(jnp.int32, g.shape, 0)
        g1 = jnp.where(row == 0, halo[7:8], pltpu.roll(g, 1, axis=0))
        g2 = jnp.where(row == 0, halo[6:7], jnp.where(row == 1, halo[7:8], pltpu.roll(g, 2, axis=0)))
        halo_ref[...] = g[sub - 8:sub]
        gc = cb_ref[0] + cw[0:1] * g2 + cw[1:2] * g1 + cw[2:3] * g
        h_ref[rows, :] = (gc * _sigmoid(gc) * u).astype(h_ref.dtype)


def _ffn1(xb, w_gate, w_up, conv_w, conv_b, layer, tm=2048, tn=512):
    s, dm = xb.shape
    dff = w_gate.shape[2]
    tm = min(tm, s)
    wspec = pl.BlockSpec((1, dm, tn), lambda j, m: (layer, 0, j))
    return pl.pallas_call(
        _ffn1_kernel,
        grid=(dff // tn, s // tm),
        in_specs=[pl.BlockSpec((tm, dm), lambda j, m: (m, 0)), wspec, wspec,
                  pl.BlockSpec((1, CONV_W, tn), lambda j, m: (layer, 0, j)),
                  pl.BlockSpec((1, 1, tn), lambda j, m: (layer, 0, j))],
        out_specs=pl.BlockSpec((tm, tn), lambda j, m: (m, j)),
        out_shape=jax.ShapeDtypeStruct((s, dff), BF16),
        scratch_shapes=[pltpu.VMEM((dm, tn), BF16), pltpu.VMEM((dm, tn), BF16),
                        pltpu.VMEM((8, tn), F32)],
        compiler_params=_params("arbitrary", "arbitrary"),
        name="ffn_gate_up_conv",
    )(xb, w_gate, w_up, conv_w, conv_b)


def _ffn2_kernel(h_ref, w_ref, x_ref, g_ref, b_ref, xo_ref, xb_ref):
    k = pl.program_id(1)

    @pl.when(k == 0)
    def _():
        xo_ref[...] = ALPHA * x_ref[...]

    slab = 512
    for c in range(xo_ref.shape[1] // slab):
        cs = slice(c * slab, (c + 1) * slab)
        xo_ref[:, cs] += _dot(h_ref[...], w_ref[0, :, cs])

    @pl.when(k == pl.num_programs(1) - 1)
    def _():
        xn = _layernorm(xo_ref[...], g_ref[...], b_ref[...])
        xo_ref[...] = xn
        xb_ref[...] = xn.astype(BF16)


def _ffn2(h, w_down, layer, x, g, b, tm=1024, tk=512):
    s, dm = x.shape
    dff = h.shape[1]
    tm = min(tm, s)
    row = pl.BlockSpec((tm, dm), lambda m, k: (m, 0))
    vec = pl.BlockSpec((1, dm), lambda m, k: (0, 0))
    return pl.pallas_call(
        _ffn2_kernel,
        grid=(s // tm, dff // tk),
        in_specs=[pl.BlockSpec((tm, tk), lambda m, k: (m, k)),
                  pl.BlockSpec((1, tk, dm), lambda m, k: (layer, k, 0)), row, vec, vec],
        out_specs=[row, row],
        out_shape=[jax.ShapeDtypeStruct((s, dm), F32), jax.ShapeDtypeStruct((s, dm), BF16)],
        compiler_params=_params("parallel", "arbitrary"),
        name="ffn_down_ln",
    )(h, w_down, x, g, b)


def _band_buckets(dilation):
    qi = jnp.arange(A_BLOCK)[None, :]
    ci = jnp.arange(2 * A_BLOCK)[:, None]
    off = A_BLOCK + qi - ci
    valid = (off >= 0) & (off <= A_SPAN)
    dist = dilation * jnp.clip(off, 0, A_SPAN)
    max_exact = N_BUCKETS // 2
    df = jnp.maximum(dist, 1).astype(F32)
    large = max_exact + (jnp.log(df / max_exact) / math.log(MAX_DIST / max_exact)
                         * (N_BUCKETS - max_exact)).astype(jnp.int32)
    bucket = jnp.where(dist < max_exact, dist, jnp.minimum(large, N_BUCKETS - 1))
    return jnp.where(valid, bucket, -1).astype(jnp.int32)


def _bias_kernel(tab_ref, idx_ref, o_ref):
    g = pl.program_id(0)
    idx = idx_ref[0]
    for h in range(A_HEADS):
        acc = jnp.full(idx.shape, NEG, F32)
        for b in range(N_BUCKETS):
            acc = jnp.where(idx == b, tab_ref[b, g * A_HEADS + h], acc)
        o_ref[h] = acc


def _band_bias(rel_bias):
    idx = jnp.stack([_band_buckets(d) for d in DILATIONS])
    n_groups = len(DILATIONS)
    return pl.pallas_call(
        _bias_kernel,
        grid=(n_groups,),
        in_specs=[pl.BlockSpec(memory_space=pltpu.SMEM),
                  pl.BlockSpec((1, 2 * A_BLOCK, A_BLOCK), lambda g: (g, 0, 0))],
        out_specs=pl.BlockSpec((A_HEADS, 2 * A_BLOCK, A_BLOCK), lambda g: (g, 0, 0)),
        out_shape=jax.ShapeDtypeStruct((n_groups * A_HEADS, 2 * A_BLOCK, A_BLOCK), F32),
        compiler_params=_params("parallel"),
        name="band_bias",
    )(rel_bias, idx)


def kernel(x, w_in, gla_gate_w, gla_gate_b, gla_norm_g, w_proj_a, w_proj_b, w_out, rel_bias,
           ln1_g, ln1_b, ffn_w_gate, ffn_w_up, ffn_conv_w, ffn_conv_b, ffn_w_down, ln2_g, ln2_b):
    _, s, dm = x.shape
    dk = gla_gate_w.shape[2]
    dv = w_proj_b.shape[1]
    n_groups = len(DILATIONS)
    a_cols = 3 * A_WIDTH * n_groups
    gla_cols = 2 * dk + 2 * dv
    low0 = a_cols + gla_cols
    bias = _band_bias(rel_bias)
    w_t = jnp.swapaxes(w_in, 1, 2)

    w_a, w_o, w_d = w_proj_a.astype(BF16), w_out.astype(BF16), ffn_w_down.astype(BF16)

    xf = x.reshape(s, dm)
    xb = xf.astype(BF16)
    for i in range(DEPTH):
        gate_w = jnp.pad(gla_gate_w[i], ((0, LANES - G_RANK), (0, 0)))
        outs, lses = [], []
        for g, d in enumerate(DILATIONS):
            qkv = _proj(xb, w_t, i, 3 * A_WIDTH * g, 3 * A_WIDTH, d, BF16)
            o, lse = _attention(qkv, bias, g)
            outs.append(o)
            lses.append(lse)
        ya = _combine(outs, lses, w_a, i)
        gla_in = _proj(xb, w_t, i, a_cols, gla_cols, 1, F32)
        cum = _gate(xb, w_t, i, low0, gate_w, gla_gate_b[i].reshape(1, dk))
        a = _gla(gla_in, cum, gla_norm_g[i].reshape(1, -1))
        gates = _proj(xb, w_t, i, low0 + G_RANK, 2 * dm, 1, BF16, act="sigmoid")
        y = _merge(a, w_proj_b, i, ya, gates)
        xf, xb = _out_ln(y, w_o, i, xf, ln1_g[i].reshape(1, dm), ln1_b[i].reshape(1, dm))
        h = _ffn1(xb, ffn_w_gate, ffn_w_up, ffn_conv_w, ffn_conv_b.reshape(DEPTH, 1, -1), i)
        xf, xb = _ffn2(h, w_d, i, xf, ln2_g[i].reshape(1, dm), ln2_b[i].reshape(1, dm))
    return xf.reshape(x.shape)
```
